```python
import jax, jax.numpy as jnp
from jax import lax
import numpy as np

D_MODEL = 1024
BATCH = 16
SEQ = 2048
DEPTH = 2

CTX_LEN = 256
GRID_W = 64
FOURIER_WIDTH = 512
FOURIER_GROUPS = 4
LRU_WIDTH = 512
LRU_HEADS = 8
LRU_HEAD_DIM = LRU_WIDTH // LRU_HEADS
LRU_CONV = 4
LRU_C = 8.0
CONF_WIDTH = 512
CONF_KERNEL = 31
FFN_HIDDEN = 2816
FFN_CONV = 3
N_BRANCH = 3
EPS = 1e-6
LN_EPS = 1e-5

OFF_F = 0
OFF_LX = OFF_F + FOURIER_WIDTH
OFF_LG = OFF_LX + LRU_WIDTH
OFF_C = OFF_LG + LRU_WIDTH
OFF_G = OFF_C + 2 * CONF_WIDTH
IN_WIDTH = OFF_G + N_BRANCH * D_MODEL

kernel_name = "hybrid_fourier_rglru_conformer_dit"


def rms_norm(x, g):
    xf = x.astype(jnp.float32)
    y = xf * lax.rsqrt(jnp.mean(xf * xf, axis=-1, keepdims=True) + EPS)
    return (y * g.astype(jnp.float32)).astype(x.dtype)


def layer_norm(x, g, b):
    xf = x.astype(jnp.float32)
    mu = jnp.mean(xf, axis=-1, keepdims=True)
    var = jnp.mean(jnp.square(xf - mu), axis=-1, keepdims=True)
    y = (xf - mu) * lax.rsqrt(var + LN_EPS)
    return (y * g.astype(jnp.float32) + b.astype(jnp.float32)).astype(x.dtype)


def modulate(h, shift, scale):
    return h * (1.0 + scale) + shift


def dwconv(x, w, b, left, rows):
    n, L, C = x.shape
    K = w.shape[0]
    if rows is not None:
        x = x.reshape(n * rows, L // rows, C)
    y = lax.conv_general_dilated(x, w[:, None, :], window_strides=(1,),
                                 padding=[(left, K - 1 - left)],
                                 dimension_numbers=('NWC', 'WIO', 'NWC'),
                                 feature_group_count=C)
    return y.reshape(n, L, C) + b


def fourier_mix(u):
    n, L, _ = u.shape
    z = u.astype(jnp.float32).reshape(n, L, FOURIER_GROUPS, FOURIER_WIDTH // FOURIER_GROUPS)
    z = jnp.fft.fft2(z, axes=(1, 3), norm="ortho").real
    return z.reshape(n, L, FOURIER_WIDTH).astype(u.dtype)


def linear_scan(a, b, h0, reverse):
    if h0 is not None:
        if reverse:
            b = b.at[:, -1].add(a[:, -1] * h0)
        else:
            b = b.at[:, 0].add(a[:, 0] * h0)

    def comb(l, r):
        a_l, b_l = l
        a_r, b_r = r
        return a_l * a_r, a_r * b_l + b_r

    _, h = lax.associative_scan(comb, (a, b), reverse=reverse, axis=1)
    return h


def rglru_scan(u, p, rows, h0f, h0b):
    xc = dwconv(u, p['lru_conv_w'], p['lru_conv_b'], LRU_CONV // 2, rows)
    n, L, _ = xc.shape
    xf = xc.astype(jnp.float32)
    xh = xf.reshape(n, L, LRU_HEADS, LRU_HEAD_DIM)

    def direction(d, h0, reverse):
        r = jax.nn.sigmoid(jnp.einsum('blhi,hij->blhj', xh, p['lru_wa'][d].astype(jnp.float32)).reshape(n, L, LRU_WIDTH)
                           + p['lru_ba'][d].astype(jnp.float32))
        i = jax.nn.sigmoid(jnp.einsum('blhi,hij->blhj', xh, p['lru_wx'][d].astype(jnp.float32)).reshape(n, L, LRU_WIDTH)
                           + p['lru_bx'][d].astype(jnp.float32))
        log_a = -LRU_C * r * jax.nn.softplus(-p['lru_lam'][d].astype(jnp.float32))
        a = jnp.exp(log_a)
        b = jnp.sqrt(-jnp.expm1(2.0 * log_a)) * (i * xf)
        return linear_scan(a, b, h0, reverse)

    hf = direction(0, h0f, False)
    hb = direction(1, h0b, True)
    return hf, hb


def token_mixer(proj, p, rows, h0f, h0b):
    n, L, _ = proj.shape
    u_f = proj[..., OFF_F:OFF_LX]
    u_x = proj[..., OFF_LX:OFF_LG]
    u_g = proj[..., OFF_LG:OFF_C]
    u_c = proj[..., OFF_C:OFF_G]
    gates = jax.nn.sigmoid(proj[..., OFF_G:]).reshape(n, L, N_BRANCH, D_MODEL)

    y_f = fourier_mix(u_f) @ p['fourier_out_w']

    hf, hb = rglru_scan(u_x, p, rows, h0f, h0b)
    y_r = ((hf + hb) * jax.nn.gelu(u_g.astype(jnp.float32))).astype(proj.dtype) @ p['lru_out_w']

    a, g = jnp.split(u_c, 2, axis=-1)
    v = a * jax.nn.sigmoid(g)
    v = dwconv(v, p['conf_conv_w'], p['conf_conv_b'], CONF_KERNEL // 2, rows)
    v = jax.nn.silu(layer_norm(v, p['conf_ln_g'], p['conf_ln_b']))
    y_c = v @ p['conf_out_w']

    merged = gates[..., 0, :] * y_f + gates[..., 1, :] * y_r + gates[..., 2, :] * y_c
    return merged @ p['mix_out_w'], hf, hb


def conv_ffn(h, p, rows):
    u = h @ p['ffn_up_w']
    u = dwconv(u, p['ffn_conv_w'], p['ffn_conv_b'], FFN_CONV // 2, rows)
    v, g = jnp.split(u, 2, axis=-1)
    return (jax.nn.silu(g) * v) @ p['ffn_down_w']


def setup_inputs(seed: int = 0) -> dict:
    key = jax.random.key(seed)
    ks = iter(jax.random.split(key, 40))
    f32 = jnp.float32

    def nrm(shape, scale):
        return jax.random.normal(next(ks), shape, f32) * scale

    L = DEPTH
    u = jax.random.uniform(next(ks), (L, 2, LRU_WIDTH), f32, minval=0.9, maxval=0.999)
    a0 = u ** (1.0 / LRU_C)
    lam = jnp.log(a0) - jnp.log1p(-a0)
    return {
        "x": nrm((BATCH, SEQ, D_MODEL), 1.0),
        "c": nrm((BATCH, D_MODEL), 1.0),
        "ctx": nrm((BATCH, CTX_LEN, D_MODEL), 1.0),
        "c_ctx": nrm((D_MODEL,), 1.0),
        "ada_w": nrm((L, D_MODEL, 6 * D_MODEL), D_MODEL ** -0.5),
        "ada_b": nrm((L, 6 * D_MODEL), 0.02),
        "norm1_g": 1.0 + nrm((L, D_MODEL), 0.05),
        "norm2_g": 1.0 + nrm((L, D_MODEL), 0.05),
        "in_w": nrm((L, D_MODEL, IN_WIDTH), D_MODEL ** -0.5),
        "in_b": nrm((L, IN_WIDTH), 0.02),
        "fourier_out_w": nrm((L, FOURIER_WIDTH, D_MODEL), FOURIER_WIDTH ** -0.5),
        "lru_conv_w": nrm((L, LRU_CONV, LRU_WIDTH), LRU_CONV ** -0.5),
        "lru_conv_b": nrm((L, LRU_WIDTH), 0.02),
        "lru_wa": nrm((L, 2, LRU_HEADS, LRU_HEAD_DIM, LRU_HEAD_DIM), LRU_HEAD_DIM ** -0.5),
        "lru_ba": nrm((L, 2, LRU_WIDTH), 0.02),
        "lru_wx": nrm((L, 2, LRU_HEADS, LRU_HEAD_DIM, LRU_HEAD_DIM), LRU_HEAD_DIM ** -0.5),
        "lru_bx": nrm((L, 2, LRU_WIDTH), 0.02),
        "lru_lam": lam,
        "lru_out_w": nrm((L, LRU_WIDTH, D_MODEL), LRU_WIDTH ** -0.5),
        "conf_conv_w": nrm((L, CONF_KERNEL, CONF_WIDTH), CONF_KERNEL ** -0.5),
        "conf_conv_b": nrm((L, CONF_WIDTH), 0.02),
        "conf_ln_g": 1.0 + nrm((L, CONF_WIDTH), 0.05),
        "conf_ln_b": nrm((L, CONF_WIDTH), 0.02),
        "conf_out_w": nrm((L, CONF_WIDTH, D_MODEL), CONF_WIDTH ** -0.5),
        "mix_out_w": nrm((L, D_MODEL, D_MODEL), D_MODEL ** -0.5),
        "ffn_up_w": nrm((L, D_MODEL, 2 * FFN_HIDDEN), D_MODEL ** -0.5),
        "ffn_conv_w": nrm((L, FFN_CONV, 2 * FFN_HIDDEN), FFN_CONV ** -0.5),
        "ffn_conv_b": nrm((L, 2 * FFN_HIDDEN), 0.02),
        "ffn_down_w": nrm((L, FFN_HIDDEN, D_MODEL), FFN_HIDDEN ** -0.5),
        "final_g": 1.0 + nrm((D_MODEL,), 0.05),
    }


def reference(x, c, ctx, c_ctx, ada_w, ada_b, norm1_g, norm2_g, in_w, in_b, fourier_out_w,
              lru_conv_w, lru_conv_b, lru_wa, lru_ba, lru_wx, lru_bx, lru_lam, lru_out_w,
              conf_conv_w, conf_conv_b, conf_ln_g, conf_ln_b, conf_out_w, mix_out_w,
              ffn_up_w, ffn_conv_w, ffn_conv_b, ffn_down_w, final_g):
    rows = x.shape[1] // GRID_W
    h_lat = x
    h_ctx = ctx
    for i in range(DEPTH):
        p = dict(fourier_out_w=fourier_out_w[i], lru_conv_w=lru_conv_w[i], lru_conv_b=lru_conv_b[i],
                 lru_wa=lru_wa[i], lru_ba=lru_ba[i], lru_wx=lru_wx[i], lru_bx=lru_bx[i],
                 lru_lam=lru_lam[i], lru_out_w=lru_out_w[i], conf_conv_w=conf_conv_w[i],
                 conf_conv_b=conf_conv_b[i], conf_ln_g=conf_ln_g[i], conf_ln_b=conf_ln_b[i],
                 conf_out_w=conf_out_w[i], mix_out_w=mix_out_w[i], ffn_up_w=ffn_up_w[i],
                 ffn_conv_w=ffn_conv_w[i], ffn_conv_b=ffn_conv_b[i], ffn_down_w=ffn_down_w[i])
        last = i == DEPTH - 1

        mod_lat = (jax.nn.silu(c) @ ada_w[i] + ada_b[i])[:, None, :]
        mod_ctx = jax.nn.silu(c_ctx) @ ada_w[i] + ada_b[i]
        sh1, sc1, g1, sh2, sc2, g2 = jnp.split(mod_lat, 6, axis=-1)
        csh1, csc1, cg1, csh2, csc2, cg2 = jnp.split(mod_ctx, 6, axis=-1)

        n_ctx = modulate(rms_norm(h_ctx, norm1_g[i]), csh1, csc1)
        if last:
            u_ctx = n_ctx @ in_w[i][:, OFF_LX:OFF_LG] + in_b[i][OFF_LX:OFF_LG]
            hf_c, hb_c = rglru_scan(u_ctx, p, None, None, None)
        else:
            proj_c = n_ctx @ in_w[i] + in_b[i]
            mix_c, hf_c, hb_c = token_mixer(proj_c, p, None, None, None)
            h_ctx = h_ctx + cg1 * mix_c
            h_ctx = h_ctx + cg2 * conv_ffn(modulate(rms_norm(h_ctx, norm2_g[i]), csh2, csc2), p, None)

        n_lat = modulate(rms_norm(h_lat, norm1_g[i]), sh1, sc1)
        proj = n_lat @ in_w[i] + in_b[i]
        mix, _, _ = token_mixer(proj, p, rows, hf_c[:, -1], hb_c[:, 0])
        h_lat = h_lat + g1 * mix
        h_lat = h_lat + g2 * conv_ffn(modulate(rms_norm(h_lat, norm2_g[i]), sh2, sc2), p, rows)
    return rms_norm(h_lat, final_g)
```

```python
import functools

import numpy as np
import jax
import jax.numpy as jnp
from jax import lax
from jax.experimental import pallas as pl
from jax.experimental.pallas import tpu as pltpu

F32 = jnp.float32
BF16 = jnp.bfloat16

GRID_W = 64
FOURIER_GROUPS = 4
LRU_HEADS = 8
LRU_C = 8.0
N_BRANCH = 3
EPS = 1e-6
LN_EPS = 1e-5

VMEM_LIMIT_BYTES = 56 * 1024 * 1024
SUBLANES = 8
CONV_CHUNK = 64
PAD_ROWS = 16
TOKEN_TILE = 256
MOD_ROWS = 24


def _const_spec(shape):
    nd = len(shape)
    return pl.BlockSpec(shape, lambda *_: (0,) * nd, pipeline_mode=pl.Buffered(1))


def _params(n_grid):
    return pltpu.CompilerParams(dimension_semantics=("arbitrary",) * n_grid,
                                vmem_limit_bytes=VMEM_LIMIT_BYTES)


def _dot(a, b):
    return jnp.dot(a, b, preferred_element_type=F32)


def _rms(x, g):
    return x * lax.rsqrt(jnp.mean(x * x, axis=-1, keepdims=True) + EPS) * g


def _ada_body(c_ref, w_ref, b_ref, o_ref):
    c = c_ref[...]
    s = c * jax.nn.sigmoid(c)
    o_ref[0] = jnp.dot(s, w_ref[0], preferred_element_type=F32,
                       precision=lax.Precision.HIGHEST) + b_ref[0]


def _ada_call(cs, ada_w, ada_b):
    depth, d, n = ada_w.shape
    nblk = 1536
    return pl.pallas_call(
        _ada_body,
        grid=(depth, n // nblk),
        in_specs=[pl.BlockSpec((MOD_ROWS, d), lambda l, j: (0, 0)),
                  pl.BlockSpec((1, d, nblk), lambda l, j: (l, 0, j)),
                  pl.BlockSpec((1, 1, nblk), lambda l, j: (l, 0, j))],
        out_specs=pl.BlockSpec((1, MOD_ROWS, nblk), lambda l, j: (l, 0, j)),
        out_shape=jax.ShapeDtypeStruct((depth, MOD_ROWS, n), F32),
        compiler_params=_params(2),
        name="adaln_mod",
    )(cs, ada_w, ada_b.reshape(depth, 1, n))


def _row_conv(pad_ref, val, w_ref, n_taps, left, row):
    tl, c = val.shape
    r = tl // row
    pad_ref[:, PAD_ROWS:PAD_ROWS + row, :] = val.reshape(r, row, c)
    pieces = []
    for ri in range(r):
        for p in range(row // CONV_CHUNK):
            base = PAD_ROWS + p * CONV_CHUNK - left
            acc = pad_ref[ri, base:base + CONV_CHUNK, :] * w_ref[0:1, :]
            for j in range(1, n_taps):
                acc = acc + pad_ref[ri, base + j:base + j + CONV_CHUNK, :] * w_ref[j:j + 1, :]
            pieces.append(acc)
    return jnp.concatenate(pieces, axis=0)


def _mixer_local_body(h_ref, mod_ref, g_ref, w_ref, b_ref, cs_ref, lcw_ref, lcb_ref,
                      ccw_ref, ccb_ref, lng_ref, lnb_ref, cow_ref,
                      zcs_ref, xc_ref, ug_ref, g01_ref, ycg_ref, pad_ref, *, row, dims):
    d, fw, lw, cw = dims
    off_lx = fw
    off_lg = off_lx + lw
    off_c = off_lg + lw
    off_g = off_c + 2 * cw
    tl = h_ref.shape[1]
    r = tl // row

    zeros = jnp.zeros((r, PAD_ROWS, pad_ref.shape[2]), F32)
    pad_ref[:, 0:PAD_ROWS, :] = zeros
    pad_ref[:, PAD_ROWS + row:, :] = zeros

    x = h_ref[0]
    mod = mod_ref[0]
    n = _rms(x, g_ref[...]) * (1.0 + mod[:, d:2 * d]) + mod[:, 0:d]
    nb = n.astype(BF16)

    def proj(lo, hi):
        return _dot(nb, w_ref[:, lo:hi]) + b_ref[:, lo:hi]

    zcs_ref[0] = _dot(proj(0, fw).astype(BF16), cs_ref[...]).astype(zcs_ref.dtype)

    xc_ref[0] = _row_conv(pad_ref, proj(off_lx, off_lg), lcw_ref, lcw_ref.shape[0],
                          lcw_ref.shape[0] // 2, row) + lcb_ref[...]
    ug_ref[0] = proj(off_lg, off_c)

    uc = proj(off_c, off_g)
    v = uc[:, :cw] * jax.nn.sigmoid(uc[:, cw:])
    v = _row_conv(pad_ref, v, ccw_ref, ccw_ref.shape[0], ccw_ref.shape[0] // 2, row) + ccb_ref[...]
    mu = jnp.mean(v, axis=-1, keepdims=True)
    vc = v - mu
    var = jnp.mean(vc * vc, axis=-1, keepdims=True)
    v = vc * lax.rsqrt(var + LN_EPS) * lng_ref[...] + lnb_ref[...]
    v = v * jax.nn.sigmoid(v)
    yc = _dot(v.astype(BF16), cow_ref[...])

    g01_ref[0] = jax.nn.sigmoid(proj(off_g, off_g + 2 * d))
    ycg_ref[0] = jax.nn.sigmoid(proj(off_g + 2 * d, off_g + 3 * d)) * yc


def _mixer_local_call(h, mod, lw, row, shared_mod):
    b, l, d = h.shape
    fw, lwid, cw = lw["fw"], lw["lw"], lw["cw"]
    tl = min(TOKEN_TILE, l)
    nt = l // tl
    mod_map = (lambda bi, ti: (0, 0, 0)) if shared_mod else (lambda bi, ti: (bi, 0, 0))
    tok = lambda width: pl.BlockSpec((1, tl, width), lambda bi, ti: (bi, ti, 0))
    body = functools.partial(_mixer_local_body, row=row, dims=(d, fw, lwid, cw))
    consts = [lw["norm1_g"], lw["in_w"], lw["in_b"], lw["chan_dft"], lw["lru_conv_w"], lw["lru_conv_b"],
              lw["conf_conv_w"], lw["conf_conv_b"], lw["conf_ln_g"], lw["conf_ln_b"], lw["conf_out_w"]]
    return pl.pallas_call(
        body,
        grid=(b, nt),
        in_specs=[tok(d), pl.BlockSpec((1, 1, mod.shape[2]), mod_map)] + [_const_spec(a.shape) for a in consts],
        out_specs=[tok(2 * fw), tok(lwid), tok(lwid), tok(2 * d), tok(d)],
        out_shape=[jax.ShapeDtypeStruct((b, l, 2 * fw), BF16),
                   jax.ShapeDtypeStruct((b, l, lwid), F32),
                   jax.ShapeDtypeStruct((b, l, lwid), F32),
                   jax.ShapeDtypeStruct((b, l, 2 * d), F32),
                   jax.ShapeDtypeStruct((b, l, d), F32)],
        scratch_shapes=[pltpu.VMEM((tl // row, row + 2 * PAD_ROWS, max(lwid, cw)), F32)],
        compiler_params=_params(2),
        name="mixer_local",
    )(h, mod, *consts)


def _fourier_seq_body(z_ref, dc_ref, ds_ref, o_ref, *, chunk):
    l = z_ref.shape[1]
    fw = o_ref.shape[2]
    for i in range(l // chunk):
        rows = slice(i * chunk, (i + 1) * chunk)
        acc = _dot(dc_ref[rows, :], z_ref[0, :, 0:fw]) + _dot(ds_ref[rows, :], z_ref[0, :, fw:2 * fw])
        o_ref[0, rows, :] = acc.astype(o_ref.dtype)


def _fourier_seq_call(zcs, dc, ds):
    b, l, w2 = zcs.shape
    fw = w2 // 2
    return pl.pallas_call(
        functools.partial(_fourier_seq_body, chunk=min(256, l)),
        grid=(b,),
        in_specs=[pl.BlockSpec((1, l, w2), lambda bi: (bi, 0, 0)), _const_spec(dc.shape), _const_spec(ds.shape)],
        out_specs=pl.BlockSpec((1, l, fw), lambda bi: (bi, 0, 0)),
        out_shape=jax.ShapeDtypeStruct((b, l, fw), BF16),
        compiler_params=_params(1),
        name="fourier_seq",
    )(zcs, dc, ds)


def _block_scan(a, b, carry, row_id, reverse):
    for dist in (1, 2, 4):
        if reverse:
            keep = row_id < SUBLANES - dist
            shift = SUBLANES - dist
        else:
            keep = row_id >= dist
            shift = dist
        a_sh = jnp.where(keep, pltpu.roll(a, shift, 0), 1.0)
        b_sh = jnp.where(keep, pltpu.roll(b, shift, 0), 0.0)
        b = a * b_sh + b
        a = a * a_sh
    return a * carry + b


def _lru_scan_body(xc_ref, ug_ref, h0_ref, wg_ref, bg_ref, lam_ref, yr_ref, hfin_ref,
                   af_ref, bf_ref, ab_ref, bb_ref, *, chunk):
    l, w = xc_ref.shape[1], xc_ref.shape[2]
    lam = lam_ref[...]
    neg = -lam
    softplus = jnp.maximum(neg, 0.0) + jnp.log1p(jnp.exp(-jnp.abs(neg)))

    for i in range(l // chunk):
        rows = slice(i * chunk, (i + 1) * chunk)
        x = xc_ref[0, rows, :]
        pre = _dot(x.astype(BF16), wg_ref[...]) + bg_ref[...]
        for di, (a_ref, b_ref) in enumerate(((af_ref, bf_ref), (ab_ref, bb_ref))):
            r_gate = jax.nn.sigmoid(pre[:, (2 * di) * w:(2 * di + 1) * w])
            i_gate = jax.nn.sigmoid(pre[:, (2 * di + 1) * w:(2 * di + 2) * w])
            a = jnp.exp(-LRU_C * r_gate * softplus[di:di + 1, :])
            a_ref[rows, :] = a
            b_ref[rows, :] = jnp.sqrt(1.0 - a * a) * (i_gate * x)

    nblk = l // SUBLANES
    row_id = lax.broadcasted_iota(jnp.int32, (SUBLANES, w), 0)
    h0 = h0_ref[0]

    def step(i, carry):
        cf, cb = carry
        rf = pl.multiple_of(i * SUBLANES, SUBLANES)
        rb = pl.multiple_of((nblk - 1 - i) * SUBLANES, SUBLANES)
        hf = _block_scan(af_ref[pl.ds(rf, SUBLANES), :], bf_ref[pl.ds(rf, SUBLANES), :], cf, row_id, False)
        hb = _block_scan(ab_ref[pl.ds(rb, SUBLANES), :], bb_ref[pl.ds(rb, SUBLANES), :], cb, row_id, True)
        bf_ref[pl.ds(rf, SUBLANES), :] = hf
        bb_ref[pl.ds(rb, SUBLANES), :] = hb
        cf = jnp.broadcast_to(hf[SUBLANES - 1:SUBLANES, :], (SUBLANES, w))
        cb = jnp.broadcast_to(hb[0:1, :], (SUBLANES, w))
        return cf, cb

    init = (jnp.broadcast_to(h0[0:1, :], (SUBLANES, w)), jnp.broadcast_to(h0[1:2, :], (SUBLANES, w)))
    lax.fori_loop(0, nblk, step, init, unroll=2)

    for i in range(l // chunk):
        rows = slice(i * chunk, (i + 1) * chunk)
        y = (bf_ref[rows, :] + bb_ref[rows, :]) * jax.nn.gelu(ug_ref[0, rows, :])
        yr_ref[0, rows, :] = y.astype(yr_ref.dtype)
    hfin_ref[0, 0:1, :] = bf_ref[l - 1:l, :]
    hfin_ref[0, 1:2, :] = bb_ref[0:1, :]


def _lru_scan_call(xc, ug, h0, lw):
    b, l, w = xc.shape
    seq = lambda width: pl.BlockSpec((1, l, width), lambda bi: (bi, 0, 0))
    st = pl.BlockSpec((1, 2, w), lambda bi: (bi, 0, 0))
    consts = [lw["lru_gate_w"], lw["lru_gate_b"], lw["lru_lam"]]
    return pl.pallas_call(
        functools.partial(_lru_scan_body, chunk=min(256, l)),
        grid=(b,),
        in_specs=[seq(w), seq(w), st] + [_const_spec(a.shape) for a in consts],
        out_specs=[seq(w), st],
        out_shape=[jax.ShapeDtypeStruct((b, l, w), BF16), jax.ShapeDtypeStruct((b, 2, w), F32)],
        scratch_shapes=[pltpu.VMEM((l, w), F32)] * 4,
        compiler_params=_params(1),
        name="lru_scan",
    )(xc, ug, h0, *consts)


def _merge_ffn_body(h_ref, f_ref, yr_ref, g01_ref, ycg_ref, mod_ref, g2_ref, fow_ref, low_ref, mow_ref,
                    upw_ref, fcw_ref, fcb_ref, dww_ref, fin_ref, o_ref, *, row, hidden_chunk, final_norm):
    tl, d = h_ref.shape[1], h_ref.shape[2]
    hidden = dww_ref.shape[0]
    mod = mod_ref[0]
    gate1, sh2, sc2, gate2 = (mod[:, 2 * d:3 * d], mod[:, 3 * d:4 * d], mod[:, 4 * d:5 * d], mod[:, 5 * d:6 * d])

    yf = _dot(f_ref[0], fow_ref[...])
    yr = _dot(yr_ref[0], low_ref[...])
    merged = g01_ref[0, :, 0:d] * yf + g01_ref[0, :, d:2 * d] * yr + ycg_ref[0]
    h1 = h_ref[0] + gate1 * _dot(merged.astype(BF16), mow_ref[...])

    nb = (_rms(h1, g2_ref[...]) * (1.0 + sc2) + sh2).astype(BF16)

    pos = lax.broadcasted_iota(jnp.int32, (tl, hidden_chunk), 0) % row
    has_prev = pos != 0
    has_next = pos != row - 1

    def conv3(u, lo):
        w = fcw_ref[:, lo:lo + hidden_chunk]
        prev = jnp.where(has_prev, pltpu.roll(u, 1, 0), 0.0)
        nxt = jnp.where(has_next, pltpu.roll(u, tl - 1, 0), 0.0)
        return prev * w[0:1, :] + u * w[1:2, :] + nxt * w[2:3, :] + fcb_ref[:, lo:lo + hidden_chunk]

    acc = jnp.zeros((tl, d), F32)
    for j in range(hidden // hidden_chunk):
        lo = j * hidden_chunk
        v = conv3(_dot(nb, upw_ref[:, lo:lo + hidden_chunk]), lo)
        g = conv3(_dot(nb, upw_ref[:, hidden + lo:hidden + lo + hidden_chunk]), hidden + lo)
        act = g * jax.nn.sigmoid(g) * v
        acc = acc + _dot(act.astype(BF16), dww_ref[lo:lo + hidden_chunk, :])
    h2 = h1 + gate2 * acc
    if final_norm:
        h2 = _rms(h2, fin_ref[...])
    o_ref[0] = h2


def _merge_ffn_call(h, f, yr, g01, ycg, mod, lw, row, shared_mod, final_norm):
    b, l, d = h.shape
    tl = min(TOKEN_TILE, l)
    nt = l // tl
    mod_map = (lambda bi, ti: (0, 0, 0)) if shared_mod else (lambda bi, ti: (bi, 0, 0))
    tok = lambda width: pl.BlockSpec((1, tl, width), lambda bi, ti: (bi, ti, 0))
    consts = [lw["norm2_g"], lw["fourier_out_w"], lw["lru_out_w"], lw["mix_out_w"], lw["ffn_up_w"],
              lw["ffn_conv_w"], lw["ffn_conv_b"], lw["ffn_down_w"], lw["final_g"]]
    body = functools.partial(_merge_ffn_body, row=row, hidden_chunk=256, final_norm=final_norm)
    return pl.pallas_call(
        body,
        grid=(b, nt),
        in_specs=[tok(d), tok(f.shape[2]), tok(yr.shape[2]), tok(2 * d), tok(d),
                  pl.BlockSpec((1, 1, mod.shape[2]), mod_map)] + [_const_spec(a.shape) for a in consts],
        out_specs=tok(d),
        out_shape=jax.ShapeDtypeStruct((b, l, d), F32),
        compiler_params=_params(2),
        name="merge_ffn",
    )(h, f, yr, g01, ycg, mod, *consts)


def _dft_constants(l, group_width, groups):
    k = np.arange(l, dtype=np.int64)
    ang = 2.0 * np.pi * ((k[:, None] * k[None, :]) % l).astype(np.float64) / l
    scale = 1.0 / np.sqrt(float(l) * group_width)
    dc = np.cos(ang) * scale
    ds = -np.sin(ang) * scale
    c = np.arange(group_width, dtype=np.int64)
    cang = 2.0 * np.pi * ((c[:, None] * c[None, :]) % group_width).astype(np.float64) / group_width
    eye = np.eye(groups)
    chan = np.concatenate([np.kron(eye, np.cos(cang)), np.kron(eye, np.sin(cang))], axis=1)
    return jnp.asarray(dc, BF16), jnp.asarray(ds, BF16), jnp.asarray(chan, BF16)


def _block_diag(w):
    h, i, j = w.shape
    eye = jnp.eye(h, dtype=w.dtype)
    return (eye[:, None, :, None] * w[:, :, None, :]).reshape(h * i, h * j)


def kernel(x, c, ctx, c_ctx, ada_w, ada_b, norm1_g, norm2_g, in_w, in_b, fourier_out_w, lru_conv_w, lru_conv_b, lru_wa, lru_ba, lru_wx, lru_bx, lru_lam, lru_out_w, conf_conv_w, conf_conv_b, conf_ln_g, conf_ln_b, conf_out_w, mix_out_w, ffn_up_w, ffn_conv_w, ffn_conv_b, ffn_down_w, final_g):
    batch, seq, d = x.shape
    ctx_len = ctx.shape[1]
    depth = ada_w.shape[0]
    fw = fourier_out_w.shape[1]
    lwid = lru_out_w.shape[1]
    cw = conf_out_w.shape[1]
    assert batch + 1 <= MOD_ROWS and seq % GRID_W == 0

    cs = jnp.zeros((MOD_ROWS, d), F32).at[:batch].set(c).at[batch].set(c_ctx)
    mod = _ada_call(cs, ada_w, ada_b)

    chan_dft = _dft_constants(seq, fw // FOURIER_GROUPS, FOURIER_GROUPS)[2]
    dft = {seq: _dft_constants(seq, fw // FOURIER_GROUPS, FOURIER_GROUPS)[:2],
           ctx_len: _dft_constants(ctx_len, fw // FOURIER_GROUPS, FOURIER_GROUPS)[:2]}

    row2 = lambda a: a.reshape(1, -1)
    h_lat, h_ctx = x, ctx
    zero_state = jnp.zeros((batch, 2, lwid), F32)
    for i in range(depth):
        gate_w = jnp.concatenate([_block_diag(lru_wa[i, 0]), _block_diag(lru_wx[i, 0]),
                                  _block_diag(lru_wa[i, 1]), _block_diag(lru_wx[i, 1])], axis=1)
        gate_b = jnp.concatenate([lru_ba[i, 0], lru_bx[i, 0], lru_ba[i, 1], lru_bx[i, 1]])
        lw = dict(fw=fw, lw=lwid, cw=cw,
                  norm1_g=row2(norm1_g[i]), norm2_g=row2(norm2_g[i]), in_w=in_w[i].astype(BF16), in_b=row2(in_b[i]),
                  chan_dft=chan_dft, lru_conv_w=lru_conv_w[i], lru_conv_b=row2(lru_conv_b[i]),
                  conf_conv_w=conf_conv_w[i], conf_conv_b=row2(conf_conv_b[i]),
                  conf_ln_g=row2(conf_ln_g[i]), conf_ln_b=row2(conf_ln_b[i]), conf_out_w=conf_out_w[i].astype(BF16),
                  lru_gate_w=gate_w.astype(BF16), lru_gate_b=row2(gate_b), lru_lam=lru_lam[i],
                  fourier_out_w=fourier_out_w[i].astype(BF16), lru_out_w=lru_out_w[i].astype(BF16),
                  mix_out_w=mix_out_w[i].astype(BF16), ffn_up_w=ffn_up_w[i].astype(BF16),
                  ffn_conv_w=ffn_conv_w[i], ffn_conv_b=row2(ffn_conv_b[i]), ffn_down_w=ffn_down_w[i].astype(BF16),
                  final_g=row2(final_g))
        last = i == depth - 1
        mod_lat = mod[i, :batch].reshape(batch, 1, -1)
        mod_ctx = mod[i, batch:batch + 1].reshape(1, 1, -1)

        zcs_c, xc_c, ug_c, g01_c, ycg_c = _mixer_local_call(h_ctx, mod_ctx, lw, ctx_len, True)
        yr_c, state_c = _lru_scan_call(xc_c, ug_c, zero_state, lw)
        if not last:
            f_c = _fourier_seq_call(zcs_c, *dft[ctx_len])
            h_ctx = _merge_ffn_call(h_ctx, f_c, yr_c, g01_c, ycg_c, mod_ctx, lw, ctx_len, True, False)

        zcs, xc, ug, g01, ycg = _mixer_local_call(h_lat, mod_lat, lw, GRID_W, False)
        f = _fourier_seq_call(zcs, *dft[seq])
        yr, _ = _lru_scan_call(xc, ug, state_c, lw)
        h_lat = _merge_ffn_call(h_lat, f, yr, g01, ycg, mod_lat, lw, GRID_W, False, last)
    return h_lat
```

```python
import functools

import numpy as np
import jax
import jax.numpy as jnp
from jax import lax
from jax.experimental import pallas as pl
from jax.experimental.pallas import tpu as pltpu

F32 = jnp.float32
BF16 = jnp.bfloat16

GRID_W = 64
FOURIER_GROUPS = 4
LRU_HEADS = 8
LRU_C = 8.0
N_BRANCH = 3
EPS = 1e-6
LN_EPS = 1e-5

VMEM_LIMIT_BYTES = 56 * 1024 * 1024
SUBLANES = 8
CONV_CHUNK = 64
PAD_ROWS = 16
TOKEN_TILE = 512
HIDDEN_CHUNK = 256
MOD_ROWS = 24


def _const_spec(shape):
    nd = len(shape)
    return pl.BlockSpec(shape, lambda *_: (0,) * nd, pipeline_mode=pl.Buffered(1))


def _params(n_grid):
    return pltpu.CompilerParams(dimension_semantics=("arbitrary",) * n_grid,
                                vmem_limit_bytes=VMEM_LIMIT_BYTES)


def _dot(a, b):
    return jnp.dot(a, b, preferred_element_type=F32)


def _rms(x, g):
    return x * lax.rsqrt(jnp.mean(x * x, axis=-1, keepdims=True) + EPS) * g


def _ada_body(c_ref, w_ref, b_ref, o_ref):
    c = c_ref[...]
    s = c * jax.nn.sigmoid(c)
    o_ref[0] = jnp.dot(s, w_ref[0], preferred_element_type=F32,
                       precision=lax.Precision.HIGHEST) + b_ref[0]


def _ada_call(cs, ada_w, ada_b):
    depth, d, n = ada_w.shape
    nblk = 1536
    return pl.pallas_call(
        _ada_body,
        grid=(depth, n // nblk),
        in_specs=[pl.BlockSpec((MOD_ROWS, d), lambda l, j: (0, 0)),
                  pl.BlockSpec((1, d, nblk), lambda l, j: (l, 0, j)),
                  pl.BlockSpec((1, 1, nblk), lambda l, j: (l, 0, j))],
        out_specs=pl.BlockSpec((1, MOD_ROWS, nblk), lambda l, j: (l, 0, j)),
        out_shape=jax.ShapeDtypeStruct((depth, MOD_ROWS, n), F32),
        compiler_params=_params(2),
        name="adaln_mod",
    )(cs, ada_w, ada_b.reshape(depth, 1, n))


def _fill_padded(pad_ref, val, row):
    tl, c = val.shape
    pad_ref[:, PAD_ROWS:PAD_ROWS + row, :] = val.reshape(tl // row, row, c)


def _row_conv_short(pad_ref, val, w_ref, left, row):
    _fill_padded(pad_ref, val, row)
    pieces = []
    for ri in range(val.shape[0] // row):
        for p in range(row // CONV_CHUNK):
            base = PAD_ROWS + p * CONV_CHUNK - left
            acc = pad_ref[ri, base:base + CONV_CHUNK, :] * w_ref[0:1, :]
            for j in range(1, w_ref.shape[0]):
                acc = acc + pad_ref[ri, base + j:base + j + CONV_CHUNK, :] * w_ref[j:j + 1, :]
            pieces.append(acc)
    return jnp.concatenate(pieces, axis=0)


def _row_conv_wide(pad_ref, shift_ref, val, w_ref, left, row):
    _fill_padded(pad_ref, val, row)
    span = shift_ref.shape[1]
    pieces = []
    for ri in range(val.shape[0] // row):
        for s in range(1, SUBLANES):
            shift_ref[s - 1] = pad_ref[ri, s:s + span, :]
        for p in range(row // CONV_CHUNK):
            acc = None
            for j in range(w_ref.shape[0]):
                q, s = divmod(PAD_ROWS - left + j + p * CONV_CHUNK, SUBLANES)
                rows = slice(q * SUBLANES, q * SUBLANES + CONV_CHUNK)
                src = pad_ref[ri, rows, :] if s == 0 else shift_ref[s - 1, rows, :]
                term = src * w_ref[j:j + 1, :]
                acc = term if acc is None else acc + term
            pieces.append(acc)
    return jnp.concatenate(pieces, axis=0)


def _zero_margins(pad_ref, row):
    zeros = jnp.zeros((pad_ref.shape[0], PAD_ROWS, pad_ref.shape[2]), F32)
    pad_ref[:, 0:PAD_ROWS, :] = zeros
    pad_ref[:, PAD_ROWS + row:, :] = zeros


def _normed_input(h_ref, mod_ref, g_ref, d):
    mod = mod_ref[0]
    n = _rms(h_ref[0], g_ref[...]) * (1.0 + mod[:, d:2 * d]) + mod[:, 0:d]
    return n.astype(BF16)


def _mixer_local_body(h_ref, mod_ref, g_ref, w_ref, b_ref, cs_ref, lcw_ref, lcb_ref,
                      ccw_ref, ccb_ref, lng_ref, lnb_ref, cow_ref,
                      zcs_ref, xc_ref, ug_ref, g01_ref, ycg_ref, pad_ref, shift_ref, *, row, dims):
    d, fw, lw, cw = dims
    off_lx = fw
    off_lg = off_lx + lw
    off_c = off_lg + lw
    off_g = off_c + 2 * cw

    _zero_margins(pad_ref, row)
    nb = _normed_input(h_ref, mod_ref, g_ref, d)

    def proj(lo, hi):
        return _dot(nb, w_ref[:, lo:hi]) + b_ref[:, lo:hi]

    zcs_ref[0] = _dot(proj(0, fw).astype(BF16), cs_ref[...]).astype(zcs_ref.dtype)

    xc = _row_conv_short(pad_ref, proj(off_lx, off_lg), lcw_ref, lcw_ref.shape[0] // 2, row) + lcb_ref[...]
    xc_ref[0] = xc.astype(xc_ref.dtype)
    ug_ref[0] = proj(off_lg, off_c).astype(ug_ref.dtype)

    uc = proj(off_c, off_g)
    v = uc[:, :cw] * jax.nn.sigmoid(uc[:, cw:])
    v = _row_conv_wide(pad_ref, shift_ref, v, ccw_ref, ccw_ref.shape[0] // 2, row) + ccb_ref[...]
    mu = jnp.mean(v, axis=-1, keepdims=True)
    vc = v - mu
    var = jnp.mean(vc * vc, axis=-1, keepdims=True)
    v = vc * lax.rsqrt(var + LN_EPS) * lng_ref[...] + lnb_ref[...]
    v = v * jax.nn.sigmoid(v)
    yc = _dot(v.astype(BF16), cow_ref[...])

    g01_ref[0] = jax.nn.sigmoid(proj(off_g, off_g + 2 * d)).astype(g01_ref.dtype)
    ycg_ref[0] = (jax.nn.sigmoid(proj(off_g + 2 * d, off_g + 3 * d)) * yc).astype(ycg_ref.dtype)


def _lru_input_body(h_ref, mod_ref, g_ref, w_ref, b_ref, lcw_ref, lcb_ref, xc_ref, pad_ref, *, row, d):
    _zero_margins(pad_ref, row)
    nb = _normed_input(h_ref, mod_ref, g_ref, d)
    ux = _dot(nb, w_ref[...]) + b_ref[...]
    xc = _row_conv_short(pad_ref, ux, lcw_ref, lcw_ref.shape[0] // 2, row) + lcb_ref[...]
    xc_ref[0] = xc.astype(xc_ref.dtype)


def _token_specs(l, shared_mod, mod):
    tl = min(TOKEN_TILE, l)
    mod_map = (lambda bi, ti: (0, 0, 0)) if shared_mod else (lambda bi, ti: (bi, 0, 0))
    tok = lambda width: pl.BlockSpec((1, tl, width), lambda bi, ti: (bi, ti, 0))
    return tl, tok, pl.BlockSpec((1, 1, mod.shape[2]), mod_map)


def _mixer_local_call(h, mod, lw, row, shared_mod):
    b, l, d = h.shape
    fw, lwid, cw = lw["fw"], lw["lw"], lw["cw"]
    tl, tok, mod_spec = _token_specs(l, shared_mod, mod)
    body = functools.partial(_mixer_local_body, row=row, dims=(d, fw, lwid, cw))
    consts = [lw["norm1_g"], lw["in_w"], lw["in_b"], lw["chan_dft"], lw["lru_conv_w"], lw["lru_conv_b"],
              lw["conf_conv_w"], lw["conf_conv_b"], lw["conf_ln_g"], lw["conf_ln_b"], lw["conf_out_w"]]
    widths = [2 * fw, lwid, lwid, 2 * d, d]
    return pl.pallas_call(
        body,
        grid=(b, l // tl),
        in_specs=[tok(d), mod_spec] + [_const_spec(a.shape) for a in consts],
        out_specs=[tok(w) for w in widths],
        out_shape=[jax.ShapeDtypeStruct((b, l, w), BF16) for w in widths],
        scratch_shapes=[pltpu.VMEM((tl // row, row + 2 * PAD_ROWS, max(lwid, cw)), F32),
                        pltpu.VMEM((SUBLANES - 1, row + 2 * PAD_ROWS - SUBLANES, cw), F32)],
        compiler_params=_params(2),
        name="mixer_local",
    )(h, mod, *consts)


def _lru_input_call(h, mod, lw, row, shared_mod):
    b, l, d = h.shape
    fw, lwid = lw["fw"], lw["lw"]
    tl, tok, mod_spec = _token_specs(l, shared_mod, mod)
    consts = [lw["norm1_g"], lw["in_w"][:, fw:fw + lwid], lw["in_b"][:, fw:fw + lwid],
              lw["lru_conv_w"], lw["lru_conv_b"]]
    return pl.pallas_call(
        functools.partial(_lru_input_body, row=row, d=d),
        grid=(b, l // tl),
        in_specs=[tok(d), mod_spec] + [_const_spec(a.shape) for a in consts],
        out_specs=tok(lwid),
        out_shape=jax.ShapeDtypeStruct((b, l, lwid), BF16),
        scratch_shapes=[pltpu.VMEM((tl // row, row + 2 * PAD_ROWS, lwid), F32)],
        compiler_params=_params(2),
        name="lru_input",
    )(h, mod, *consts)


def _fourier_seq_body(z_ref, dc_ref, ds_ref, o_ref, *, chunk):
    l = z_ref.shape[1]
    fw = o_ref.shape[2]
    for i in range(l // chunk):
        rows = slice(i * chunk, (i + 1) * chunk)
        acc = _dot(dc_ref[rows, :], z_ref[0, :, 0:fw]) + _dot(ds_ref[rows, :], z_ref[0, :, fw:2 * fw])
        o_ref[0, rows, :] = acc.astype(o_ref.dtype)


def _fourier_seq_call(zcs, dc, ds):
    b, l, w2 = zcs.shape
    fw = w2 // 2
    return pl.pallas_call(
        functools.partial(_fourier_seq_body, chunk=min(256, l)),
        grid=(b,),
        in_specs=[pl.BlockSpec((1, l, w2), lambda bi: (bi, 0, 0)), _const_spec(dc.shape), _const_spec(ds.shape)],
        out_specs=pl.BlockSpec((1, l, fw), lambda bi: (bi, 0, 0)),
        out_shape=jax.ShapeDtypeStruct((b, l, fw), BF16),
        compiler_params=_params(1),
        name="fourier_seq",
    )(zcs, dc, ds)


def _block_scan(a, b, carry, row_id, reverse):
    for dist in (1, 2, 4):
        if reverse:
            keep = row_id < SUBLANES - dist
            shift = SUBLANES - dist
        else:
            keep = row_id >= dist
            shift = dist
        a_sh = jnp.where(keep, pltpu.roll(a, shift, 0), 1.0)
        b_sh = jnp.where(keep, pltpu.roll(b, shift, 0), 0.0)
        b = a * b_sh + b
        a = a * a_sh
    return a * carry + b


def _lru_scan_body(*refs, chunk, with_output):
    if with_output:
        xc_ref, ug_ref, h0_ref, wg_ref, bg_ref, lam_ref, yr_ref, hfin_ref, af_ref, bf_ref, ab_ref, bb_ref = refs
    else:
        xc_ref, h0_ref, wg_ref, bg_ref, lam_ref, hfin_ref, af_ref, bf_ref, ab_ref, bb_ref = refs
    l, w = xc_ref.shape[1], xc_ref.shape[2]
    neg = -lam_ref[...]
    softplus = jnp.maximum(neg, 0.0) + jnp.log1p(jnp.exp(-jnp.abs(neg)))

    for i in range(l // chunk):
        rows = slice(i * chunk, (i + 1) * chunk)
        xb = xc_ref[0, rows, :]
        x = xb.astype(F32)
        pre = _dot(xb, wg_ref[...]) + bg_ref[...]
        for di, (a_ref, b_ref) in enumerate(((af_ref, bf_ref), (ab_ref, bb_ref))):
            r_gate = jax.nn.sigmoid(pre[:, (2 * di) * w:(2 * di + 1) * w])
            i_gate = jax.nn.sigmoid(pre[:, (2 * di + 1) * w:(2 * di + 2) * w])
            a = jnp.exp(-LRU_C * r_gate * softplus[di:di + 1, :])
            a_ref[rows, :] = a
            b_ref[rows, :] = jnp.sqrt(1.0 - a * a) * (i_gate * x)

    nblk = l // SUBLANES
    row_id = lax.broadcasted_iota(jnp.int32, (SUBLANES, w), 0)
    h0 = h0_ref[0]

    def step(i, carry):
        cf, cb = carry
        rf = pl.multiple_of(i * SUBLANES, SUBLANES)
        rb = pl.multiple_of((nblk - 1 - i) * SUBLANES, SUBLANES)
        hf = _block_scan(af_ref[pl.ds(rf, SUBLANES), :], bf_ref[pl.ds(rf, SUBLANES), :], cf, row_id, False)
        hb = _block_scan(ab_ref[pl.ds(rb, SUBLANES), :], bb_ref[pl.ds(rb, SUBLANES), :], cb, row_id, True)
        bf_ref[pl.ds(rf, SUBLANES), :] = hf
        bb_ref[pl.ds(rb, SUBLANES), :] = hb
        cf = jnp.broadcast_to(hf[SUBLANES - 1:SUBLANES, :], (SUBLANES, w))
        cb = jnp.broadcast_to(hb[0:1, :], (SUBLANES, w))
        return cf, cb

    init = (jnp.broadcast_to(h0[0:1, :], (SUBLANES, w)), jnp.broadcast_to(h0[1:2, :], (SUBLANES, w)))
    lax.fori_loop(0, nblk, step, init, unroll=2)

    if with_output:
        for i in range(l // chunk):
            rows = slice(i * chunk, (i + 1) * chunk)
            y = (bf_ref[rows, :] + bb_ref[rows, :]) * jax.nn.gelu(ug_ref[0, rows, :].astype(F32))
            yr_ref[0, rows, :] = y.astype(yr_ref.dtype)
    hfin_ref[0, 0:1, :] = bf_ref[l - 1:l, :]
    hfin_ref[0, 1:2, :] = bb_ref[0:1, :]


def _lru_scan_call(xc, ug, h0, lw):
    b, l, w = xc.shape
    seq = pl.BlockSpec((1, l, w), lambda bi: (bi, 0, 0))
    st = pl.BlockSpec((1, 2, w), lambda bi: (bi, 0, 0))
    consts = [lw["lru_gate_w"], lw["lru_gate_b"], lw["lru_lam"]]
    with_output = ug is not None
    st_shape = jax.ShapeDtypeStruct((b, 2, w), F32)
    out = pl.pallas_call(
        functools.partial(_lru_scan_body, chunk=min(256, l), with_output=with_output),
        grid=(b,),
        in_specs=[seq] * (2 if with_output else 1) + [st] + [_const_spec(a.shape) for a in consts],
        out_specs=[seq, st] if with_output else st,
        out_shape=[jax.ShapeDtypeStruct((b, l, w), BF16), st_shape] if with_output else st_shape,
        scratch_shapes=[pltpu.VMEM((l, w), F32)] * 4,
        compiler_params=_params(1),
        name="lru_scan",
    )(*([xc, ug] if with_output else [xc]), h0, *consts)
    return out if with_output else (None, out)


def _merge_ffn_body(h_ref, f_ref, yr_ref, g01_ref, ycg_ref, mod_ref, g2_ref, fow_ref, low_ref, mow_ref,
                    upw_ref, wprev_ref, wmid_ref, wnext_ref, fcb_ref, dww_ref, fin_ref, o_ref, act_ref,
                    *, row, final_norm):
    tl, d = h_ref.shape[1], h_ref.shape[2]
    hidden = dww_ref.shape[0]
    hc = HIDDEN_CHUNK
    r = tl // row
    mod = mod_ref[0]
    gate1, sh2, sc2, gate2 = (mod[:, 2 * d:3 * d], mod[:, 3 * d:4 * d], mod[:, 4 * d:5 * d], mod[:, 5 * d:6 * d])

    yf = _dot(f_ref[0], fow_ref[...])
    yr = _dot(yr_ref[0], low_ref[...])
    merged = g01_ref[0, :, 0:d] * yf + g01_ref[0, :, d:2 * d] * yr + ycg_ref[0]
    h1 = h_ref[0] + gate1 * _dot(merged.astype(BF16), mow_ref[...])

    nb = (_rms(h1, g2_ref[...]) * (1.0 + sc2) + sh2).astype(BF16)

    def conv3(u, lo):
        cols = slice(lo, lo + hc)
        prev = pltpu.roll(u, 1, 0).reshape(r, row, hc) * wprev_ref[:, cols][None]
        nxt = pltpu.roll(u, tl - 1, 0).reshape(r, row, hc) * wnext_ref[:, cols][None]
        out = u.reshape(r, row, hc) * wmid_ref[:, cols][None] + prev + nxt + fcb_ref[:, cols][None]
        return out.reshape(tl, hc)

    for j in range(hidden // hc):
        lo = j * hc
        v = conv3(_dot(nb, upw_ref[:, lo:lo + hc]), lo)
        g = conv3(_dot(nb, upw_ref[:, hidden + lo:hidden + lo + hc]), hidden + lo)
        act_ref[:, lo:lo + hc] = (g * jax.nn.sigmoid(g) * v).astype(act_ref.dtype)
    h2 = h1 + gate2 * _dot(act_ref[...], dww_ref[...])
    if final_norm:
        h2 = _rms(h2, fin_ref[...])
    o_ref[0] = h2


def _merge_ffn_call(h, f, yr, g01, ycg, mod, lw, row, shared_mod, final_norm):
    b, l, d = h.shape
    tl, tok, mod_spec = _token_specs(l, shared_mod, mod)
    conv_w = lw["ffn_conv_w"]
    pos = jnp.arange(row)[:, None]
    wprev = jnp.where(pos != 0, conv_w[0:1], 0.0)
    wnext = jnp.where(pos != row - 1, conv_w[2:3], 0.0)
    consts = [lw["norm2_g"], lw["fourier_out_w"], lw["lru_out_w"], lw["mix_out_w"], lw["ffn_up_w"],
              wprev, conv_w[1:2], wnext, lw["ffn_conv_b"], lw["ffn_down_w"], lw["final_g"]]
    hidden = lw["ffn_down_w"].shape[0]
    return pl.pallas_call(
        functools.partial(_merge_ffn_body, row=row, final_norm=final_norm),
        grid=(b, l // tl),
        in_specs=[tok(d), tok(f.shape[2]), tok(yr.shape[2]), tok(2 * d), tok(d), mod_spec]
        + [_const_spec(a.shape) for a in consts],
        out_specs=tok(d),
        out_shape=jax.ShapeDtypeStruct((b, l, d), F32),
        scratch_shapes=[pltpu.VMEM((tl, hidden), BF16)],
        compiler_params=_params(2),
        name="merge_ffn",
    )(h, f, yr, g01, ycg, mod, *consts)


def _dft_constants(l, group_width, groups):
    k = np.arange(l, dtype=np.int64)
    ang = 2.0 * np.pi * ((k[:, None] * k[None, :]) % l).astype(np.float64) / l
    scale = 1.0 / np.sqrt(float(l) * group_width)
    dc = np.cos(ang) * scale
    ds = -np.sin(ang) * scale
    c = np.arange(group_width, dtype=np.int64)
    cang = 2.0 * np.pi * ((c[:, None] * c[None, :]) % group_width).astype(np.float64) / group_width
    eye = np.eye(groups)
    chan = np.concatenate([np.kron(eye, np.cos(cang)), np.kron(eye, np.sin(cang))], axis=1)
    return jnp.asarray(dc, BF16), jnp.asarray(ds, BF16), jnp.asarray(chan, BF16)


def _block_diag(w):
    h, i, j = w.shape
    eye = jnp.eye(h, dtype=w.dtype)
    return (eye[:, None, :, None] * w[:, :, None, :]).reshape(h * i, h * j)


def kernel(x, c, ctx, c_ctx, ada_w, ada_b, norm1_g, norm2_g, in_w, in_b, fourier_out_w, lru_conv_w, lru_conv_b, lru_wa, lru_ba, lru_wx, lru_bx, lru_lam, lru_out_w, conf_conv_w, conf_conv_b, conf_ln_g, conf_ln_b, conf_out_w, mix_out_w, ffn_up_w, ffn_conv_w, ffn_conv_b, ffn_down_w, final_g):
    batch, seq, d = x.shape
    ctx_len = ctx.shape[1]
    depth = ada_w.shape[0]
    fw = fourier_out_w.shape[1]
    lwid = lru_out_w.shape[1]
    cw = conf_out_w.shape[1]
    assert batch + 1 <= MOD_ROWS and seq % GRID_W == 0

    cs = jnp.zeros((MOD_ROWS, d), F32).at[:batch].set(c).at[batch].set(c_ctx)
    mod = _ada_call(cs, ada_w, ada_b)

    chan_dft = _dft_constants(seq, fw // FOURIER_GROUPS, FOURIER_GROUPS)[2]
    dft = {seq: _dft_constants(seq, fw // FOURIER_GROUPS, FOURIER_GROUPS)[:2],
           ctx_len: _dft_constants(ctx_len, fw // FOURIER_GROUPS, FOURIER_GROUPS)[:2]}

    row2 = lambda a: a.reshape(1, -1)
    h_lat, h_ctx = x, ctx
    zero_state = jnp.zeros((batch, 2, lwid), F32)
    for i in range(depth):
        gate_w = jnp.concatenate([_block_diag(lru_wa[i, 0]), _block_diag(lru_wx[i, 0]),
                                  _block_diag(lru_wa[i, 1]), _block_diag(lru_wx[i, 1])], axis=1)
        gate_b = jnp.concatenate([lru_ba[i, 0], lru_bx[i, 0], lru_ba[i, 1], lru_bx[i, 1]])
        lw = dict(fw=fw, lw=lwid, cw=cw,
                  norm1_g=row2(norm1_g[i]), norm2_g=row2(norm2_g[i]), in_w=in_w[i].astype(BF16), in_b=row2(in_b[i]),
                  chan_dft=chan_dft, lru_conv_w=lru_conv_w[i], lru_conv_b=row2(lru_conv_b[i]),
                  conf_conv_w=conf_conv_w[i], conf_conv_b=row2(conf_conv_b[i]),
                  conf_ln_g=row2(conf_ln_g[i]), conf_ln_b=row2(conf_ln_b[i]), conf_out_w=conf_out_w[i].astype(BF16),
                  lru_gate_w=gate_w.astype(BF16), lru_gate_b=row2(gate_b), lru_lam=lru_lam[i],
                  fourier_out_w=fourier_out_w[i].astype(BF16), lru_out_w=lru_out_w[i].astype(BF16),
                  mix_out_w=mix_out_w[i].astype(BF16), ffn_up_w=ffn_up_w[i].astype(BF16),
                  ffn_conv_w=ffn_conv_w[i], ffn_conv_b=row2(ffn_conv_b[i]), ffn_down_w=ffn_down_w[i].astype(BF16),
                  final_g=row2(final_g))
        last = i == depth - 1
        mod_lat = mod[i, :batch].reshape(batch, 1, -1)
        mod_ctx = mod[i, batch:batch + 1].reshape(1, 1, -1)

        if last:
            xc_c = _lru_input_call(h_ctx, mod_ctx, lw, ctx_len, True)
            _, state_c = _lru_scan_call(xc_c, None, zero_state, lw)
        else:
            zcs_c, xc_c, ug_c, g01_c, ycg_c = _mixer_local_call(h_ctx, mod_ctx, lw, ctx_len, True)
            yr_c, state_c = _lru_scan_call(xc_c, ug_c, zero_state, lw)
            f_c = _fourier_seq_call(zcs_c, *dft[ctx_len])
            h_ctx = _merge_ffn_call(h_ctx, f_c, yr_c, g01_c, ycg_c, mod_ctx, lw, ctx_len, True, False)

        zcs, xc, ug, g01, ycg = _mixer_local_call(h_lat, mod_lat, lw, GRID_W, False)
        f = _fourier_seq_call(zcs, *dft[seq])
        yr, _ = _lru_scan_call(xc, ug, state_c, lw)
        h_lat = _merge_ffn_call(h_lat, f, yr, g01, ycg, mod_lat, lw, GRID_W, False, last)
    return h_lat
```

```python
import functools

import numpy as np
import jax
import jax.numpy as jnp
from jax import lax
from jax.experimental import pallas as pl
from jax.experimental.pallas import tpu as pltpu

F32 = jnp.float32
BF16 = jnp.bfloat16

GRID_W = 64
FOURIER_GROUPS = 4
LRU_HEADS = 8
LRU_C = 8.0
N_BRANCH = 3
EPS = 1e-6
LN_EPS = 1e-5

VMEM_LIMIT_BYTES = 56 * 1024 * 1024
SUBLANES = 8
CONV_CHUNK = 64
PAD_ROWS = 16
TOKEN_TILE = 512
HIDDEN_CHUNK = 256
GATE_CHUNK = 256
LANES = 128
MOD_ROWS = 24


def _const_spec(shape):
    nd = len(shape)
    return pl.BlockSpec(shape, lambda *_: (0,) * nd, pipeline_mode=pl.Buffered(1))


def _params(n_grid):
    return pltpu.CompilerParams(dimension_semantics=("arbitrary",) * n_grid,
                                vmem_limit_bytes=VMEM_LIMIT_BYTES)


def _dot(a, b):
    return jnp.dot(a, b, preferred_element_type=F32)


def _gelu_tanh(x):
    c1 = float(np.sqrt(2.0 / np.pi))
    half = 0.5 * x
    return half + half * jnp.tanh(x * (c1 + (c1 * 0.044715) * (x * x)))


def _rms(x, g):
    return x * lax.rsqrt(jnp.mean(x * x, axis=-1, keepdims=True) + EPS) * g


def _ada_body(c_ref, w_ref, b_ref, o_ref):
    c = c_ref[...]
    s = c * jax.nn.sigmoid(c)
    o_ref[0] = jnp.dot(s, w_ref[0], preferred_element_type=F32,
                       precision=lax.Precision.HIGHEST) + b_ref[0]


def _ada_call(cs, ada_w, ada_b):
    depth, d, n = ada_w.shape
    nblk = 1536
    return pl.pallas_call(
        _ada_body,
        grid=(depth, n // nblk),
        in_specs=[pl.BlockSpec((MOD_ROWS, d), lambda l, j: (0, 0)),
                  pl.BlockSpec((1, d, nblk), lambda l, j: (l, 0, j)),
                  pl.BlockSpec((1, 1, nblk), lambda l, j: (l, 0, j))],
        out_specs=pl.BlockSpec((1, MOD_ROWS, nblk), lambda l, j: (l, 0, j)),
        out_shape=jax.ShapeDtypeStruct((depth, MOD_ROWS, n), F32),
        compiler_params=_params(2),
        name="adaln_mod",
    )(cs, ada_w, ada_b.reshape(depth, 1, n))


def _fill_padded(pad_ref, val, row):
    tl, c = val.shape
    pad_ref[:, PAD_ROWS:PAD_ROWS + row, :] = val.reshape(tl // row, row, c)


def _row_conv_short(pad_ref, val, w_ref, left, row):
    _fill_padded(pad_ref, val, row)
    pieces = []
    for ri in range(val.shape[0] // row):
        for p in range(row // CONV_CHUNK):
            base = PAD_ROWS + p * CONV_CHUNK - left
            acc = pad_ref[ri, base:base + CONV_CHUNK, :] * w_ref[0:1, :]
            for j in range(1, w_ref.shape[0]):
                acc = acc + pad_ref[ri, base + j:base + j + CONV_CHUNK, :] * w_ref[j:j + 1, :]
            pieces.append(acc)
    return jnp.concatenate(pieces, axis=0)


def _build_phase_copies(pad_ref, shift_ref, ri):
    span = shift_ref.shape[1]
    for s in range(1, SUBLANES):
        shift_ref[s - 1] = pad_ref[ri, s:s + span, :]


def _row_conv_wide_chunk(pad_ref, shift_ref, w_ref, left, ri, p):
    acc = None
    for j in range(w_ref.shape[0]):
        q, s = divmod(PAD_ROWS - left + j + p * CONV_CHUNK, SUBLANES)
        rows = slice(q * SUBLANES, q * SUBLANES + CONV_CHUNK)
        src = pad_ref[ri, rows, :] if s == 0 else shift_ref[s - 1, rows, :]
        term = src * w_ref[j:j + 1, :]
        acc = term if acc is None else acc + term
    return acc


def _zero_margins(pad_ref, row):
    zeros = jnp.zeros((pad_ref.shape[0], PAD_ROWS, pad_ref.shape[2]), F32)
    pad_ref[:, 0:PAD_ROWS, :] = zeros
    pad_ref[:, PAD_ROWS + row:, :] = zeros


def _normed_input(h_ref, mod_ref, g_ref, d):
    mod = mod_ref[0]
    n = _rms(h_ref[0], g_ref[...]) * (1.0 + mod[:, d:2 * d]) + mod[:, 0:d]
    return n.astype(BF16)


def _mixer_local_body(h_ref, mod_ref, g_ref, w_ref, b_ref, cs_ref, lcw_ref, lcb_ref,
                      ccw_ref, ccb_ref, lng_ref, lnb_ref, cow_ref,
                      zcs_ref, xc_ref, ug_ref, g01_ref, ycg_ref, pad_ref, shift_ref, act_ref, gate_ref,
                      *, row, dims):
    d, fw, lw, cw = dims
    off_lx = fw
    off_lg = off_lx + lw
    off_c = off_lg + lw
    off_g = off_c + 2 * cw
    tl = h_ref.shape[1]

    _zero_margins(pad_ref, row)
    nb = _normed_input(h_ref, mod_ref, g_ref, d)

    def proj(lo, hi):
        return _dot(nb, w_ref[:, lo:hi]) + b_ref[:, lo:hi]

    ux = proj(off_lx, off_lg)
    uc = proj(off_c, off_g)
    xc = _row_conv_short(pad_ref, ux, lcw_ref, lcw_ref.shape[0] // 2, row) + lcb_ref[...]
    xc_ref[0] = xc.astype(xc_ref.dtype)

    def fourier_job():
        zcs_ref[0] = _dot(proj(0, fw).astype(BF16), cs_ref[...]).astype(zcs_ref.dtype)

    def gate_branch_job():
        ug_ref[0] = proj(off_lg, off_c).astype(ug_ref.dtype)

    def gate_job(lo):
        s = jax.nn.sigmoid(proj(off_g + lo, off_g + lo + GATE_CHUNK))
        if lo < 2 * d:
            g01_ref[0, :, lo:lo + GATE_CHUNK] = s.astype(g01_ref.dtype)
        else:
            gate_ref[:, lo - 2 * d:lo - 2 * d + GATE_CHUNK] = s

    jobs = [fourier_job, gate_branch_job]
    jobs += [functools.partial(gate_job, lo) for lo in range(0, N_BRANCH * d, GATE_CHUNK)]

    _fill_padded(pad_ref, uc[:, :cw] * jax.nn.sigmoid(uc[:, cw:]), row)
    units = [(ri, p) for ri in range(tl // row) for p in range(row // CONV_CHUNK)]
    jobs_per_unit = -(-len(jobs) // len(units))
    for u, (ri, p) in enumerate(units):
        for job in jobs[u * jobs_per_unit:(u + 1) * jobs_per_unit]:
            job()
        if p == 0:
            _build_phase_copies(pad_ref, shift_ref, ri)
        v = _row_conv_wide_chunk(pad_ref, shift_ref, ccw_ref, ccw_ref.shape[0] // 2, ri, p) + ccb_ref[...]
        mu = jnp.mean(v, axis=-1, keepdims=True)
        vc = v - mu
        var = jnp.mean(vc * vc, axis=-1, keepdims=True)
        v = vc * lax.rsqrt(var + LN_EPS) * lng_ref[...] + lnb_ref[...]
        act_ref[u * CONV_CHUNK:(u + 1) * CONV_CHUNK, :] = (v * jax.nn.sigmoid(v)).astype(act_ref.dtype)

    yc = _dot(act_ref[...], cow_ref[...])
    ycg_ref[0] = (gate_ref[...] * yc).astype(ycg_ref.dtype)


def _lru_input_body(h_ref, mod_ref, g_ref, w_ref, b_ref, lcw_ref, lcb_ref, xc_ref, pad_ref, *, row, d):
    _zero_margins(pad_ref, row)
    nb = _normed_input(h_ref, mod_ref, g_ref, d)
    ux = _dot(nb, w_ref[...]) + b_ref[...]
    xc = _row_conv_short(pad_ref, ux, lcw_ref, lcw_ref.shape[0] // 2, row) + lcb_ref[...]
    xc_ref[0] = xc.astype(xc_ref.dtype)


def _token_specs(l, shared_mod, mod):
    tl = min(TOKEN_TILE, l)
    mod_map = (lambda bi, ti: (0, 0, 0)) if shared_mod else (lambda bi, ti: (bi, 0, 0))
    tok = lambda width: pl.BlockSpec((1, tl, width), lambda bi, ti: (bi, ti, 0))
    return tl, tok, pl.BlockSpec((1, 1, mod.shape[2]), mod_map)


def _mixer_local_call(h, mod, lw, row, shared_mod):
    b, l, d = h.shape
    fw, lwid, cw = lw["fw"], lw["lw"], lw["cw"]
    tl, tok, mod_spec = _token_specs(l, shared_mod, mod)
    body = functools.partial(_mixer_local_body, row=row, dims=(d, fw, lwid, cw))
    consts = [lw["norm1_g"], lw["in_w"], lw["in_b"], lw["chan_dft"], lw["lru_conv_w"], lw["lru_conv_b"],
              lw["conf_conv_w"], lw["conf_conv_b"], lw["conf_ln_g"], lw["conf_ln_b"], lw["conf_out_w"]]
    widths = [2 * fw, lwid, lwid, 2 * d, d]
    return pl.pallas_call(
        body,
        grid=(b, l // tl),
        in_specs=[tok(d), mod_spec] + [_const_spec(a.shape) for a in consts],
        out_specs=[tok(w) for w in widths],
        out_shape=[jax.ShapeDtypeStruct((b, l, w), BF16) for w in widths],
        scratch_shapes=[pltpu.VMEM((tl // row, row + 2 * PAD_ROWS, max(lwid, cw)), F32),
                        pltpu.VMEM((SUBLANES - 1, row + 2 * PAD_ROWS - SUBLANES, cw), F32),
                        pltpu.VMEM((tl, cw), BF16),
                        pltpu.VMEM((tl, d), F32)],
        compiler_params=_params(2),
        name="mixer_local",
    )(h, mod, *consts)


def _lru_input_call(h, mod, lw, row, shared_mod):
    b, l, d = h.shape
    fw, lwid = lw["fw"], lw["lw"]
    tl, tok, mod_spec = _token_specs(l, shared_mod, mod)
    consts = [lw["norm1_g"], lw["in_w"][:, fw:fw + lwid], lw["in_b"][:, fw:fw + lwid],
              lw["lru_conv_w"], lw["lru_conv_b"]]
    return pl.pallas_call(
        functools.partial(_lru_input_body, row=row, d=d),
        grid=(b, l // tl),
        in_specs=[tok(d), mod_spec] + [_const_spec(a.shape) for a in consts],
        out_specs=tok(lwid),
        out_shape=jax.ShapeDtypeStruct((b, l, lwid), BF16),
        scratch_shapes=[pltpu.VMEM((tl // row, row + 2 * PAD_ROWS, lwid), F32)],
        compiler_params=_params(2),
        name="lru_input",
    )(h, mod, *consts)


def _segment_carries(h_fin, p_fin, h0, reverse):
    rows = [None] * SUBLANES
    c = h0
    for s in (range(SUBLANES - 1, -1, -1) if reverse else range(SUBLANES)):
        rows[s] = c
        c = h_fin[:, s:s + 1, :] + p_fin[:, s:s + 1, :] * c
    return jnp.concatenate(rows, axis=1), c


def _lru_scan_body(*refs, chunk, with_output):
    if with_output:
        xc_ref, ug_ref, h0_ref, wg_ref, bg_ref, lam_ref, yr_ref, hfin_ref, af_ref, bf_ref, ab_ref, bb_ref = refs
    else:
        xc_ref, h0_ref, wg_ref, bg_ref, lam_ref, hfin_ref, af_ref, bf_ref, ab_ref, bb_ref = refs
    l, w = xc_ref.shape[1], xc_ref.shape[2]
    groups = w // LANES
    seg = l // SUBLANES
    piece = min(chunk, seg)
    neg = -lam_ref[...]
    softplus = jnp.maximum(neg, 0.0) + jnp.log1p(jnp.exp(-jnp.abs(neg)))
    half_rate = (-0.5 * LRU_C * np.log2(np.e)) * softplus

    def seg_rows(t0):
        s, k0 = divmod(t0, seg)
        return pl.ds(SUBLANES * k0 + s, piece, stride=SUBLANES)

    def scatter(ref, val, t0):
        for q in range(val.shape[0] // piece):
            for g in range(groups):
                ref[g, seg_rows(t0 + q * piece), :] = val[q * piece:(q + 1) * piece, g * LANES:(g + 1) * LANES]

    for i in range(l // chunk):
        rows = slice(i * chunk, (i + 1) * chunk)
        xb = xc_ref[0, rows, :]
        half_x = 0.5 * xb.astype(F32)
        th = jnp.tanh(_dot(xb, wg_ref[...]) + bg_ref[...])
        for di, (a_ref, b_ref) in enumerate(((af_ref, bf_ref), (ab_ref, bb_ref))):
            rate = half_rate[di:di + 1, :]
            a = jnp.exp2(th[:, (2 * di) * w:(2 * di + 1) * w] * rate + rate)
            gated_x = (th[:, (2 * di + 1) * w:(2 * di + 2) * w] + 1.0) * half_x
            scatter(a_ref, a, i * chunk)
            y = 1.0 - a * a
            root = jnp.where(y > 0.0, y * lax.rsqrt(y), 0.0)
            scatter(b_ref, root * gated_x, i * chunk)

    def step(k, carry):
        hf, pf, hb, pb = carry
        rf = pl.ds(pl.multiple_of(k * SUBLANES, SUBLANES), SUBLANES)
        rb = pl.ds(pl.multiple_of((seg - 1 - k) * SUBLANES, SUBLANES), SUBLANES)
        a = af_ref[:, rf, :]
        hf = a * hf + bf_ref[:, rf, :]
        pf = a * pf
        bf_ref[:, rf, :] = hf
        af_ref[:, rf, :] = pf
        a = ab_ref[:, rb, :]
        hb = a * hb + bb_ref[:, rb, :]
        pb = a * pb
        bb_ref[:, rb, :] = hb
        ab_ref[:, rb, :] = pb
        return hf, pf, hb, pb

    zeros = jnp.zeros((groups, SUBLANES, LANES), F32)
    ones = jnp.ones((groups, SUBLANES, LANES), F32)
    hf, pf, hb, pb = lax.fori_loop(0, seg, step, (zeros, ones, zeros, ones), unroll=8)

    h0 = h0_ref[0]
    split = lambda row: jnp.stack([row[:, g * LANES:(g + 1) * LANES] for g in range(groups)])
    cf, end_f = _segment_carries(hf, pf, split(h0[0:1, :]), False)
    cb, end_b = _segment_carries(hb, pb, split(h0[1:2, :]), True)
    for g in range(groups):
        hfin_ref[0, 0:1, g * LANES:(g + 1) * LANES] = end_f[g]
        hfin_ref[0, 1:2, g * LANES:(g + 1) * LANES] = end_b[g]

    if with_output:
        for t0 in range(0, l, piece):
            s = t0 // seg
            for g in range(groups):
                lanes = slice(g * LANES, (g + 1) * LANES)
                h = (bf_ref[g, seg_rows(t0), :] + af_ref[g, seg_rows(t0), :] * cf[g, s:s + 1, :]
                     + bb_ref[g, seg_rows(t0), :] + ab_ref[g, seg_rows(t0), :] * cb[g, s:s + 1, :])
                y = h * _gelu_tanh(ug_ref[0, t0:t0 + piece, lanes].astype(F32))
                yr_ref[0, t0:t0 + piece, lanes] = y.astype(yr_ref.dtype)


def _lru_scan_call(xc, ug, h0, lw):
    b, l, w = xc.shape
    seq = pl.BlockSpec((1, l, w), lambda bi: (bi, 0, 0))
    st = pl.BlockSpec((1, 2, w), lambda bi: (bi, 0, 0))
    consts = [lw["lru_gate_w"], lw["lru_gate_b"], lw["lru_lam"]]
    with_output = ug is not None
    st_shape = jax.ShapeDtypeStruct((b, 2, w), F32)
    out = pl.pallas_call(
        functools.partial(_lru_scan_body, chunk=min(256, l), with_output=with_output),
        grid=(b,),
        in_specs=[seq] * (2 if with_output else 1) + [st] + [_const_spec(a.shape) for a in consts],
        out_specs=[seq, st] if with_output else st,
        out_shape=[jax.ShapeDtypeStruct((b, l, w), BF16), st_shape] if with_output else st_shape,
        scratch_shapes=[pltpu.VMEM((w // LANES, l, LANES), F32)] * 4,
        compiler_params=_params(1),
        name="lru_scan",
    )(*([xc, ug] if with_output else [xc]), h0, *consts)
    return out if with_output else (None, out)


def _fourier_seq_body(z_ref, dc_ref, ds_ref, o_ref, *, chunk):
    l = z_ref.shape[1]
    fw = o_ref.shape[2]
    for i in range(l // chunk):
        rows = slice(i * chunk, (i + 1) * chunk)
        acc = _dot(dc_ref[rows, :], z_ref[0, :, 0:fw]) + _dot(ds_ref[rows, :], z_ref[0, :, fw:2 * fw])
        o_ref[0, rows, :] = acc.astype(o_ref.dtype)


def _fourier_seq_call(zcs, dc, ds):
    b, l, w2 = zcs.shape
    fw = w2 // 2
    return pl.pallas_call(
        functools.partial(_fourier_seq_body, chunk=min(256, l)),
        grid=(b,),
        in_specs=[pl.BlockSpec((1, l, w2), lambda bi: (bi, 0, 0)), _const_spec(dc.shape), _const_spec(ds.shape)],
        out_specs=pl.BlockSpec((1, l, fw), lambda bi: (bi, 0, 0)),
        out_shape=jax.ShapeDtypeStruct((b, l, fw), BF16),
        compiler_params=_params(1),
        name="fourier_seq",
    )(zcs, dc, ds)


def _merge_ffn_body(h_ref, f_ref, yr_ref, g01_ref, ycg_ref, mod_ref, g2_ref, fow_ref, low_ref, mow_ref,
                    upw_ref, wprev_ref, wmid_ref, wnext_ref, fcb_ref, dww_ref, fin_ref, o_ref, act_ref,
                    *, row, final_norm):
    tl, d = h_ref.shape[1], h_ref.shape[2]
    hidden = dww_ref.shape[0]
    hc = HIDDEN_CHUNK
    r = tl // row
    mod = mod_ref[0]
    gate1, sh2, sc2, gate2 = (mod[:, 2 * d:3 * d], mod[:, 3 * d:4 * d], mod[:, 4 * d:5 * d], mod[:, 5 * d:6 * d])

    yf = _dot(f_ref[0], fow_ref[...])
    yr = _dot(yr_ref[0], low_ref[...])
    merged = g01_ref[0, :, 0:d] * yf + g01_ref[0, :, d:2 * d] * yr + ycg_ref[0]
    h1 = h_ref[0] + gate1 * _dot(merged.astype(BF16), mow_ref[...])

    nb = (_rms(h1, g2_ref[...]) * (1.0 + sc2) + sh2).astype(BF16)

    def conv3(u, lo):
        cols = slice(lo, lo + hc)
        prev = pltpu.roll(u, 1, 0).reshape(r, row, hc) * wprev_ref[:, cols][None]
        nxt = pltpu.roll(u, tl - 1, 0).reshape(r, row, hc) * wnext_ref[:, cols][None]
        out = u.reshape(r, row, hc) * wmid_ref[:, cols][None] + prev + nxt + fcb_ref[:, cols][None]
        return out.reshape(tl, hc)

    for j in range(hidden // hc):
        lo = j * hc
        v = conv3(_dot(nb, upw_ref[:, lo:lo + hc]), lo)
        g = conv3(_dot(nb, upw_ref[:, hidden + lo:hidden + lo + hc]), hidden + lo)
        act_ref[:, lo:lo + hc] = (g * jax.nn.sigmoid(g) * v).astype(act_ref.dtype)
    h2 = h1 + gate2 * _dot(act_ref[...], dww_ref[...])
    if final_norm:
        h2 = _rms(h2, fin_ref[...])
    o_ref[0] = h2


def _merge_ffn_call(h, f, yr, g01, ycg, mod, lw, row, shared_mod, final_norm):
    b, l, d = h.shape
    tl, tok, mod_spec = _token_specs(l, shared_mod, mod)
    conv_w = lw["ffn_conv_w"]
    pos = jnp.arange(row)[:, None]
    wprev = jnp.where(pos != 0, conv_w[0:1], 0.0)
    wnext = jnp.where(pos != row - 1, conv_w[2:3], 0.0)
    consts = [lw["norm2_g"], lw["fourier_out_w"], lw["lru_out_w"], lw["mix_out_w"], lw["ffn_up_w"],
              wprev, conv_w[1:2], wnext, lw["ffn_conv_b"], lw["ffn_down_w"], lw["final_g"]]
    hidden = lw["ffn_down_w"].shape[0]
    return pl.pallas_call(
        functools.partial(_merge_ffn_body, row=row, final_norm=final_norm),
        grid=(b, l // tl),
        in_specs=[tok(d), tok(f.shape[2]), tok(yr.shape[2]), tok(2 * d), tok(d), mod_spec]
        + [_const_spec(a.shape) for a in consts],
        out_specs=tok(d),
        out_shape=jax.ShapeDtypeStruct((b, l, d), F32),
        scratch_shapes=[pltpu.VMEM((tl, hidden), BF16)],
        compiler_params=_params(2),
        name="merge_ffn",
    )(h, f, yr, g01, ycg, mod, *consts)


def _dft_constants(l, group_width, groups):
    k = np.arange(l, dtype=np.int64)
    ang = 2.0 * np.pi * ((k[:, None] * k[None, :]) % l).astype(np.float64) / l
    scale = 1.0 / np.sqrt(float(l) * group_width)
    dc = np.cos(ang) * scale
    ds = -np.sin(ang) * scale
    c = np.arange(group_width, dtype=np.int64)
    cang = 2.0 * np.pi * ((c[:, None] * c[None, :]) % group_width).astype(np.float64) / group_width
    eye = np.eye(groups)
    chan = np.concatenate([np.kron(eye, np.cos(cang)), np.kron(eye, np.sin(cang))], axis=1)
    return jnp.asarray(dc, BF16), jnp.asarray(ds, BF16), jnp.asarray(chan, BF16)


def _block_diag(w):
    h, i, j = w.shape
    eye = jnp.eye(h, dtype=w.dtype)
    return (eye[:, None, :, None] * w[:, :, None, :]).reshape(h * i, h * j)


def kernel(x, c, ctx, c_ctx, ada_w, ada_b, norm1_g, norm2_g, in_w, in_b, fourier_out_w, lru_conv_w, lru_conv_b, lru_wa, lru_ba, lru_wx, lru_bx, lru_lam, lru_out_w, conf_conv_w, conf_conv_b, conf_ln_g, conf_ln_b, conf_out_w, mix_out_w, ffn_up_w, ffn_conv_w, ffn_conv_b, ffn_down_w, final_g):
    batch, seq, d = x.shape
    ctx_len = ctx.shape[1]
    depth = ada_w.shape[0]
    fw = fourier_out_w.shape[1]
    lwid = lru_out_w.shape[1]
    cw = conf_out_w.shape[1]
    assert batch + 1 <= MOD_ROWS and seq % GRID_W == 0

    cs = jnp.zeros((MOD_ROWS, d), F32).at[:batch].set(c).at[batch].set(c_ctx)
    mod = _ada_call(cs, ada_w, ada_b)

    chan_dft = _dft_constants(seq, fw // FOURIER_GROUPS, FOURIER_GROUPS)[2]
    dft = {seq: _dft_constants(seq, fw // FOURIER_GROUPS, FOURIER_GROUPS)[:2],
           ctx_len: _dft_constants(ctx_len, fw // FOURIER_GROUPS, FOURIER_GROUPS)[:2]}

    row2 = lambda a: a.reshape(1, -1)
    h_lat, h_ctx = x, ctx
    zero_state = jnp.zeros((batch, 2, lwid), F32)
    for i in range(depth):
        gate_w = jnp.concatenate([_block_diag(lru_wa[i, 0]), _block_diag(lru_wx[i, 0]),
                                  _block_diag(lru_wa[i, 1]), _block_diag(lru_wx[i, 1])], axis=1)
        gate_b = jnp.concatenate([lru_ba[i, 0], lru_bx[i, 0], lru_ba[i, 1], lru_bx[i, 1]])
        lw = dict(fw=fw, lw=lwid, cw=cw,
                  norm1_g=row2(norm1_g[i]), norm2_g=row2(norm2_g[i]), in_w=in_w[i].astype(BF16), in_b=row2(in_b[i]),
                  chan_dft=chan_dft, lru_conv_w=lru_conv_w[i], lru_conv_b=row2(lru_conv_b[i]),
                  conf_conv_w=conf_conv_w[i], conf_conv_b=row2(conf_conv_b[i]),
                  conf_ln_g=row2(conf_ln_g[i]), conf_ln_b=row2(conf_ln_b[i]), conf_out_w=conf_out_w[i].astype(BF16),
                  lru_gate_w=(0.5 * gate_w).astype(BF16), lru_gate_b=row2(0.5 * gate_b), lru_lam=lru_lam[i],
                  fourier_out_w=fourier_out_w[i].astype(BF16), lru_out_w=lru_out_w[i].astype(BF16),
                  mix_out_w=mix_out_w[i].astype(BF16), ffn_up_w=ffn_up_w[i].astype(BF16),
                  ffn_conv_w=ffn_conv_w[i], ffn_conv_b=row2(ffn_conv_b[i]), ffn_down_w=ffn_down_w[i].astype(BF16),
                  final_g=row2(final_g))
        last = i == depth - 1
        mod_lat = mod[i, :batch].reshape(batch, 1, -1)
        mod_ctx = mod[i, batch:batch + 1].reshape(1, 1, -1)

        flat = lambda a: a.reshape(1, batch * ctx_len, a.shape[-1])
        per_seq = lambda a: a.reshape(batch, ctx_len, a.shape[-1])
        if last:
            xc_c = _lru_input_call(flat(h_ctx), mod_ctx, lw, ctx_len, True)
            _, state_c = _lru_scan_call(per_seq(xc_c), None, zero_state, lw)
        else:
            zcs_c, xc_c, ug_c, g01_c, ycg_c = _mixer_local_call(flat(h_ctx), mod_ctx, lw, ctx_len, True)
            yr_c, state_c = _lru_scan_call(per_seq(xc_c), per_seq(ug_c), zero_state, lw)
            f_c = _fourier_seq_call(per_seq(zcs_c), *dft[ctx_len])
            h_ctx = per_seq(_merge_ffn_call(flat(h_ctx), flat(f_c), flat(yr_c), g01_c, ycg_c, mod_ctx, lw,
                                            ctx_len, True, False))

        zcs, xc, ug, g01, ycg = _mixer_local_call(h_lat, mod_lat, lw, GRID_W, False)
        f = _fourier_seq_call(zcs, *dft[seq])
        yr, _ = _lru_scan_call(xc, ug, state_c, lw)
        h_lat = _merge_ffn_call(h_lat, f, yr, g01, ycg, mod_lat, lw, GRID_W, False, last)
    return h_lat
```

```python
import functools

import numpy as np
import jax
import jax.numpy as jnp
from jax import lax
from jax.experimental import pallas as pl
from jax.experimental.pallas import tpu as pltpu

F32 = jnp.float32
BF16 = jnp.bfloat16

GRID_W = 64
FOURIER_GROUPS = 4
LRU_HEADS = 8
LRU_C = 8.0
N_BRANCH = 3
EPS = 1e-6
LN_EPS = 1e-5

VMEM_LIMIT_BYTES = 56 * 1024 * 1024
SUBLANES = 8
CONV_CHUNK = 64
PAD_ROWS = 16
TOKEN_TILE = 512
HIDDEN_CHUNK = 256
GATE_CHUNK = 256
LANES = 128
MXU_TILE = 256
MOD_ROWS = 24


def _const_spec(shape):
    nd = len(shape)
    return pl.BlockSpec(shape, lambda *_: (0,) * nd, pipeline_mode=pl.Buffered(1))


def _params(n_grid):
    return pltpu.CompilerParams(dimension_semantics=("arbitrary",) * n_grid,
                                vmem_limit_bytes=VMEM_LIMIT_BYTES)


def _dot(a, b):
    return jnp.dot(a, b, preferred_element_type=F32)


def _gelu_tanh(x):
    c1 = float(np.sqrt(2.0 / np.pi))
    half = 0.5 * x
    return half + half * jnp.tanh(x * (c1 + (c1 * 0.044715) * (x * x)))


def _rms(x, g):
    return x * lax.rsqrt(jnp.mean(x * x, axis=-1, keepdims=True) + EPS) * g


def _ada_body(c_ref, w_ref, b_ref, o_ref):
    c = c_ref[...]
    s = c * jax.nn.sigmoid(c)
    o_ref[0] = jnp.dot(s, w_ref[0], preferred_element_type=F32,
                       precision=lax.Precision.HIGHEST) + b_ref[0]


def _ada_call(cs, ada_w, ada_b):
    depth, d, n = ada_w.shape
    nblk = 1536
    return pl.pallas_call(
        _ada_body,
        grid=(depth, n // nblk),
        in_specs=[pl.BlockSpec((MOD_ROWS, d), lambda l, j: (0, 0)),
                  pl.BlockSpec((1, d, nblk), lambda l, j: (l, 0, j)),
                  pl.BlockSpec((1, 1, nblk), lambda l, j: (l, 0, j))],
        out_specs=pl.BlockSpec((1, MOD_ROWS, nblk), lambda l, j: (l, 0, j)),
        out_shape=jax.ShapeDtypeStruct((depth, MOD_ROWS, n), F32),
        compiler_params=_params(2),
        name="adaln_mod",
    )(cs, ada_w, ada_b.reshape(depth, 1, n))


def _fill_padded(pad_ref, val, row):
    tl, c = val.shape
    pad_ref[:, PAD_ROWS:PAD_ROWS + row, :] = val.reshape(tl // row, row, c)


def _row_conv_short(pad_ref, val, w_ref, left, row):
    _fill_padded(pad_ref, val, row)
    pieces = []
    for ri in range(val.shape[0] // row):
        for p in range(row // CONV_CHUNK):
            base = PAD_ROWS + p * CONV_CHUNK - left
            acc = pad_ref[ri, base:base + CONV_CHUNK, :] * w_ref[0:1, :]
            for j in range(1, w_ref.shape[0]):
                acc = acc + pad_ref[ri, base + j:base + j + CONV_CHUNK, :] * w_ref[j:j + 1, :]
            pieces.append(acc)
    return jnp.concatenate(pieces, axis=0)


def _tap_offset(left, j, p):
    return divmod(PAD_ROWS - left + j + p * CONV_CHUNK, SUBLANES)


def _build_phase_copies(pad_ref, shift_ref, ri, n_taps, left):
    span = shift_ref.shape[1]
    for s in sorted({_tap_offset(left, j, 0)[1] for j in range(n_taps)} - {0}):
        shift_ref[s - 1] = pad_ref[ri, s:s + span, :]


def _row_conv_wide_chunk(pad_ref, shift_ref, w_ref, left, ri, p):
    acc = None
    for j in range(w_ref.shape[0]):
        q, s = _tap_offset(left, j, p)
        rows = slice(q * SUBLANES, q * SUBLANES + CONV_CHUNK)
        src = pad_ref[ri, rows, :] if s == 0 else shift_ref[s - 1, rows, :]
        term = src * w_ref[j:j + 1, :]
        acc = term if acc is None else acc + term
    return acc


def _zero_margins(pad_ref, row):
    zeros = jnp.zeros((pad_ref.shape[0], PAD_ROWS, pad_ref.shape[2]), F32)
    pad_ref[:, 0:PAD_ROWS, :] = zeros
    pad_ref[:, PAD_ROWS + row:, :] = zeros


def _normed_input(h_ref, mod_ref, g_ref, d):
    mod = mod_ref[0]
    n = _rms(h_ref[0], g_ref[...]) * (1.0 + mod[:, d:2 * d]) + mod[:, 0:d]
    return n.astype(BF16)


def _mixer_local_body(h_ref, mod_ref, g_ref, w_ref, b_ref, cs_ref, lcw_ref, lcb_ref,
                      ccw_ref, ccb_ref, lng_ref, lnb_ref, cow_ref,
                      zcs_ref, xc_ref, ug_ref, g01_ref, ycg_ref, pad_ref, shift_ref, act_ref, gate_ref,
                      *, row, dims):
    d, fw, lw, cw = dims
    off_lx = fw
    off_lg = off_lx + lw
    off_c = off_lg + lw
    off_g = off_c + 2 * cw
    tl = h_ref.shape[1]

    _zero_margins(pad_ref, row)
    nb = _normed_input(h_ref, mod_ref, g_ref, d)

    def proj(lo, hi):
        return _dot(nb, w_ref[:, lo:hi]) + b_ref[:, lo:hi]

    ux = proj(off_lx, off_lg)
    uc = proj(off_c, off_g)
    units = [(ri, p) for ri in range(tl // row) for p in range(row // CONV_CHUNK)]
    _fill_padded(pad_ref, ux, row)
    for u, (ri, p) in enumerate(units):
        if p == 0:
            _build_phase_copies(pad_ref, shift_ref, ri, lcw_ref.shape[0], lcw_ref.shape[0] // 2)
        xc = _row_conv_wide_chunk(pad_ref, shift_ref, lcw_ref, lcw_ref.shape[0] // 2, ri, p) + lcb_ref[...]
        xc_ref[0, u * CONV_CHUNK:(u + 1) * CONV_CHUNK, :] = xc.astype(xc_ref.dtype)

    def fourier_job():
        uf = proj(0, fw).astype(BF16)
        kt = cs_ref.shape[0]
        for t in range(fw // kt):
            z = _dot(uf[:, t * kt:(t + 1) * kt], cs_ref[...]).astype(zcs_ref.dtype)
            zcs_ref[0, :, t * kt:(t + 1) * kt] = z[:, 0:kt]
            zcs_ref[0, :, fw + t * kt:fw + (t + 1) * kt] = z[:, kt:2 * kt]

    def gate_branch_job():
        ug_ref[0] = proj(off_lg, off_c).astype(ug_ref.dtype)

    def gate_job(lo):
        s = jax.nn.sigmoid(proj(off_g + lo, off_g + lo + GATE_CHUNK))
        if lo < 2 * d:
            g01_ref[0, :, lo:lo + GATE_CHUNK] = s.astype(g01_ref.dtype)
        else:
            gate_ref[:, lo - 2 * d:lo - 2 * d + GATE_CHUNK] = s

    jobs = [fourier_job, gate_branch_job]
    jobs += [functools.partial(gate_job, lo) for lo in range(0, N_BRANCH * d, GATE_CHUNK)]

    _fill_padded(pad_ref, uc[:, :cw] * jax.nn.sigmoid(uc[:, cw:]), row)
    jobs_per_unit = -(-len(jobs) // len(units))
    for u, (ri, p) in enumerate(units):
        for job in jobs[u * jobs_per_unit:(u + 1) * jobs_per_unit]:
            job()
        if p == 0:
            _build_phase_copies(pad_ref, shift_ref, ri, ccw_ref.shape[0], ccw_ref.shape[0] // 2)
        v = _row_conv_wide_chunk(pad_ref, shift_ref, ccw_ref, ccw_ref.shape[0] // 2, ri, p) + ccb_ref[...]
        mu = jnp.mean(v, axis=-1, keepdims=True)
        vc = v - mu
        var = jnp.mean(vc * vc, axis=-1, keepdims=True)
        v = vc * lax.rsqrt(var + LN_EPS) * lng_ref[...] + lnb_ref[...]
        act_ref[u * CONV_CHUNK:(u + 1) * CONV_CHUNK, :] = (v * jax.nn.sigmoid(v)).astype(act_ref.dtype)

    yc = _dot(act_ref[...], cow_ref[...])
    ycg_ref[0] = (gate_ref[...] * yc).astype(ycg_ref.dtype)


def _lru_input_body(h_ref, mod_ref, g_ref, w_ref, b_ref, lcw_ref, lcb_ref, xc_ref, pad_ref, *, row, d):
    _zero_margins(pad_ref, row)
    nb = _normed_input(h_ref, mod_ref, g_ref, d)
    ux = _dot(nb, w_ref[...]) + b_ref[...]
    xc = _row_conv_short(pad_ref, ux, lcw_ref, lcw_ref.shape[0] // 2, row) + lcb_ref[...]
    xc_ref[0] = xc.astype(xc_ref.dtype)


def _token_specs(l, shared_mod, mod):
    tl = min(TOKEN_TILE, l)
    mod_map = (lambda bi, ti: (0, 0, 0)) if shared_mod else (lambda bi, ti: (bi, 0, 0))
    tok = lambda width: pl.BlockSpec((1, tl, width), lambda bi, ti: (bi, ti, 0))
    return tl, tok, pl.BlockSpec((1, 1, mod.shape[2]), mod_map)


def _mixer_local_call(h, mod, lw, row, shared_mod):
    b, l, d = h.shape
    fw, lwid, cw = lw["fw"], lw["lw"], lw["cw"]
    tl, tok, mod_spec = _token_specs(l, shared_mod, mod)
    body = functools.partial(_mixer_local_body, row=row, dims=(d, fw, lwid, cw))
    consts = [lw["norm1_g"], lw["in_w"], lw["in_b"], lw["chan_dft"], lw["lru_conv_w"], lw["lru_conv_b"],
              lw["conf_conv_w"], lw["conf_conv_b"], lw["conf_ln_g"], lw["conf_ln_b"], lw["conf_out_w"]]
    widths = [2 * fw, lwid, lwid, 2 * d, d]
    return pl.pallas_call(
        body,
        grid=(b, l // tl),
        in_specs=[tok(d), mod_spec] + [_const_spec(a.shape) for a in consts],
        out_specs=[tok(w) for w in widths],
        out_shape=[jax.ShapeDtypeStruct((b, l, w), BF16) for w in widths],
        scratch_shapes=[pltpu.VMEM((tl // row, row + 2 * PAD_ROWS, max(lwid, cw)), F32),
                        pltpu.VMEM((SUBLANES - 1, row + 2 * PAD_ROWS - SUBLANES, cw), F32),
                        pltpu.VMEM((tl, cw), BF16),
                        pltpu.VMEM((tl, d), F32)],
        compiler_params=_params(2),
        name="mixer_local",
    )(h, mod, *consts)


def _lru_input_call(h, mod, lw, row, shared_mod):
    b, l, d = h.shape
    fw, lwid = lw["fw"], lw["lw"]
    tl, tok, mod_spec = _token_specs(l, shared_mod, mod)
    consts = [lw["norm1_g"], lw["in_w"][:, fw:fw + lwid], lw["in_b"][:, fw:fw + lwid],
              lw["lru_conv_w"], lw["lru_conv_b"]]
    return pl.pallas_call(
        functools.partial(_lru_input_body, row=row, d=d),
        grid=(b, l // tl),
        in_specs=[tok(d), mod_spec] + [_const_spec(a.shape) for a in consts],
        out_specs=tok(lwid),
        out_shape=jax.ShapeDtypeStruct((b, l, lwid), BF16),
        scratch_shapes=[pltpu.VMEM((tl // row, row + 2 * PAD_ROWS, lwid), F32)],
        compiler_params=_params(2),
        name="lru_input",
    )(h, mod, *consts)


def _segment_carries(h_fin, p_fin, h0, reverse):
    rows = [None] * SUBLANES
    c = h0
    for s in (range(SUBLANES - 1, -1, -1) if reverse else range(SUBLANES)):
        rows[s] = c
        c = h_fin[:, s:s + 1, :] + p_fin[:, s:s + 1, :] * c
    return jnp.concatenate(rows, axis=1), c


def _lru_scan_body(*refs, chunk, with_output):
    if with_output:
        xc_ref, ug_ref, h0_ref, wg_ref, bg_ref, lam_ref, yr_ref, hfin_ref, af_ref, bf_ref, ab_ref, bb_ref = refs
    else:
        xc_ref, h0_ref, wg_ref, bg_ref, lam_ref, hfin_ref, af_ref, bf_ref, ab_ref, bb_ref = refs
    l, w = xc_ref.shape[1], xc_ref.shape[2]
    groups = w // LANES
    seg = l // SUBLANES
    piece = min(chunk, seg)
    neg = -lam_ref[...]
    softplus = jnp.maximum(neg, 0.0) + jnp.log1p(jnp.exp(-jnp.abs(neg)))
    half_rate = (-0.5 * LRU_C * np.log2(np.e)) * softplus

    def seg_rows(t0):
        s, k0 = divmod(t0, seg)
        return pl.ds(SUBLANES * k0 + s, piece, stride=SUBLANES)

    def scatter(ref, val, t0):
        for q in range(val.shape[0] // piece):
            for g in range(groups):
                ref[g, seg_rows(t0 + q * piece), :] = val[q * piece:(q + 1) * piece, g * LANES:(g + 1) * LANES]

    for i in range(l // chunk):
        rows = slice(i * chunk, (i + 1) * chunk)
        xb = xc_ref[0, rows, :]
        half_x = 0.5 * xb.astype(F32)

        def gate_tanh(gi):
            kt = wg_ref.shape[2]
            pre = [_dot(xb[:, t * kt:(t + 1) * kt], wg_ref[gi, t]) for t in range(w // kt)]
            return jnp.tanh(jnp.concatenate(pre, axis=1) + bg_ref[:, gi * w:(gi + 1) * w])

        for di, (a_ref, b_ref) in enumerate(((af_ref, bf_ref), (ab_ref, bb_ref))):
            rate = half_rate[di:di + 1, :]
            a = jnp.exp2(gate_tanh(2 * di) * rate + rate)
            gated_x = (gate_tanh(2 * di + 1) + 1.0) * half_x
            scatter(a_ref, a, i * chunk)
            y = 1.0 - a * a
            root = jnp.where(y > 0.0, y * lax.rsqrt(y), 0.0)
            scatter(b_ref, root * gated_x, i * chunk)

    def step(k, carry):
        hf, pf, hb, pb = carry
        rf = pl.ds(pl.multiple_of(k * SUBLANES, SUBLANES), SUBLANES)
        rb = pl.ds(pl.multiple_of((seg - 1 - k) * SUBLANES, SUBLANES), SUBLANES)
        a = af_ref[:, rf, :]
        hf = a * hf + bf_ref[:, rf, :]
        pf = a * pf
        bf_ref[:, rf, :] = hf
        af_ref[:, rf, :] = pf
        a = ab_ref[:, rb, :]
        hb = a * hb + bb_ref[:, rb, :]
        pb = a * pb
        bb_ref[:, rb, :] = hb
        ab_ref[:, rb, :] = pb
        return hf, pf, hb, pb

    zeros = jnp.zeros((groups, SUBLANES, LANES), F32)
    ones = jnp.ones((groups, SUBLANES, LANES), F32)
    hf, pf, hb, pb = lax.fori_loop(0, seg, step, (zeros, ones, zeros, ones), unroll=8)

    h0 = h0_ref[0]
    split = lambda row: jnp.stack([row[:, g * LANES:(g + 1) * LANES] for g in range(groups)])
    cf, end_f = _segment_carries(hf, pf, split(h0[0:1, :]), False)
    cb, end_b = _segment_carries(hb, pb, split(h0[1:2, :]), True)
    for g in range(groups):
        hfin_ref[0, 0:1, g * LANES:(g + 1) * LANES] = end_f[g]
        hfin_ref[0, 1:2, g * LANES:(g + 1) * LANES] = end_b[g]

    if with_output:
        for t0 in range(0, l, piece):
            s = t0 // seg
            for g in range(groups):
                lanes = slice(g * LANES, (g + 1) * LANES)
                h = (bf_ref[g, seg_rows(t0), :] + af_ref[g, seg_rows(t0), :] * cf[g, s:s + 1, :]
                     + bb_ref[g, seg_rows(t0), :] + ab_ref[g, seg_rows(t0), :] * cb[g, s:s + 1, :])
                y = h * _gelu_tanh(ug_ref[0, t0:t0 + piece, lanes].astype(F32))
                yr_ref[0, t0:t0 + piece, lanes] = y.astype(yr_ref.dtype)


def _lru_scan_call(xc, ug, h0, lw):
    b, l, w = xc.shape
    seq = pl.BlockSpec((1, l, w), lambda bi: (bi, 0, 0))
    st = pl.BlockSpec((1, 2, w), lambda bi: (bi, 0, 0))
    consts = [lw["lru_gate_w"], lw["lru_gate_b"], lw["lru_lam"]]
    with_output = ug is not None
    st_shape = jax.ShapeDtypeStruct((b, 2, w), F32)
    out = pl.pallas_call(
        functools.partial(_lru_scan_body, chunk=min(256, l), with_output=with_output),
        grid=(b,),
        in_specs=[seq] * (2 if with_output else 1) + [st] + [_const_spec(a.shape) for a in consts],
        out_specs=[seq, st] if with_output else st,
        out_shape=[jax.ShapeDtypeStruct((b, l, w), BF16), st_shape] if with_output else st_shape,
        scratch_shapes=[pltpu.VMEM((w // LANES, l, LANES), F32)] * 4,
        compiler_params=_params(1),
        name="lru_scan",
    )(*([xc, ug] if with_output else [xc]), h0, *consts)
    return out if with_output else (None, out)


def _fourier_seq_body(z_ref, dc_ref, ds_ref, twc_ref, tws_ref, o_ref, fold_ref, il_ref, *, chunk):
    l, fw = o_ref.shape[1], o_ref.shape[2]
    half = l // 2
    for i in range(half // chunk):
        rows = slice(i * chunk, (i + 1) * chunk)
        lo = z_ref[0, rows, :].astype(F32)
        hi = z_ref[0, half + i * chunk:half + (i + 1) * chunk, :].astype(F32)
        tot, dif = lo + hi, lo - hi
        dre, dim = dif[:, 0:fw], dif[:, fw:2 * fw]
        cos, sin = twc_ref[rows, :], tws_ref[rows, :]
        fold_ref[0, rows, :] = tot[:, 0:fw].astype(fold_ref.dtype)
        fold_ref[1, rows, :] = tot[:, fw:2 * fw].astype(fold_ref.dtype)
        fold_ref[2, rows, :] = (dre * cos - dim * sin).astype(fold_ref.dtype)
        fold_ref[3, rows, :] = (dre * sin + dim * cos).astype(fold_ref.dtype)
    for i in range(half // chunk):
        rows = slice(i * chunk, (i + 1) * chunk)
        even = _dot(dc_ref[rows, :], fold_ref[0]) + _dot(ds_ref[rows, :], fold_ref[1])
        odd = _dot(dc_ref[rows, :], fold_ref[2]) + _dot(ds_ref[rows, :], fold_ref[3])
        for g in range(fw // LANES):
            lanes = slice(g * LANES, (g + 1) * LANES)
            il_ref[g, pl.ds(2 * i * chunk, chunk, stride=2), :] = even[:, lanes]
            il_ref[g, pl.ds(2 * i * chunk + 1, chunk, stride=2), :] = odd[:, lanes]
            out_rows = slice(2 * i * chunk, 2 * (i + 1) * chunk)
            o_ref[0, out_rows, lanes] = il_ref[g, out_rows, :].astype(o_ref.dtype)


def _fourier_seq_call(zcs, dc, ds, twc, tws):
    b, l, w2 = zcs.shape
    fw = w2 // 2
    consts = [dc, ds, twc, tws]
    return pl.pallas_call(
        functools.partial(_fourier_seq_body, chunk=min(256, l // 2)),
        grid=(b,),
        in_specs=[pl.BlockSpec((1, l, w2), lambda bi: (bi, 0, 0))] + [_const_spec(a.shape) for a in consts],
        out_specs=pl.BlockSpec((1, l, fw), lambda bi: (bi, 0, 0)),
        out_shape=jax.ShapeDtypeStruct((b, l, fw), BF16),
        scratch_shapes=[pltpu.VMEM((4, l // 2, fw), BF16), pltpu.VMEM((fw // LANES, l, LANES), F32)],
        compiler_params=_params(1),
        name="fourier_seq",
    )(zcs, *consts)


def _merge_ffn_body(h_ref, f_ref, yr_ref, g01_ref, ycg_ref, mod_ref, g2_ref, fow_ref, low_ref, mow_ref,
                    upw_ref, wprev_ref, wmid_ref, wnext_ref, fcb_ref, dww_ref, fin_ref, o_ref, act_ref,
                    *, row, final_norm):
    tl, d = h_ref.shape[1], h_ref.shape[2]
    hidden = dww_ref.shape[0]
    hc = HIDDEN_CHUNK
    r = tl // row
    mod = mod_ref[0]
    gate1, sh2, sc2, gate2 = (mod[:, 2 * d:3 * d], mod[:, 3 * d:4 * d], mod[:, 4 * d:5 * d], mod[:, 5 * d:6 * d])

    yf = _dot(f_ref[0], fow_ref[...])
    yr = _dot(yr_ref[0], low_ref[...])
    merged = g01_ref[0, :, 0:d] * yf + g01_ref[0, :, d:2 * d] * yr + ycg_ref[0]
    h1 = h_ref[0] + gate1 * _dot(merged.astype(BF16), mow_ref[...])

    nb = (_rms(h1, g2_ref[...]) * (1.0 + sc2) + sh2).astype(BF16)

    def conv3(u, lo):
        cols = slice(lo, lo + hc)
        prev = pltpu.roll(u, 1, 0).reshape(r, row, hc) * wprev_ref[:, cols][None]
        nxt = pltpu.roll(u, tl - 1, 0).reshape(r, row, hc) * wnext_ref[:, cols][None]
        out = u.reshape(r, row, hc) * wmid_ref[:, cols][None] + prev + nxt + fcb_ref[:, cols][None]
        return out.reshape(tl, hc)

    for j in range(hidden // hc):
        lo = j * hc
        v = conv3(_dot(nb, upw_ref[:, lo:lo + hc]), lo)
        g = conv3(_dot(nb, upw_ref[:, hidden + lo:hidden + lo + hc]), hidden + lo)
        act_ref[:, lo:lo + hc] = (g * jax.nn.sigmoid(g) * v).astype(act_ref.dtype)
    h2 = h1 + gate2 * _dot(act_ref[...], dww_ref[...])
    if final_norm:
        h2 = _rms(h2, fin_ref[...])
    o_ref[0] = h2


def _merge_ffn_call(h, f, yr, g01, ycg, mod, lw, row, shared_mod, final_norm):
    b, l, d = h.shape
    tl, tok, mod_spec = _token_specs(l, shared_mod, mod)
    conv_w = lw["ffn_conv_w"]
    pos = jnp.arange(row)[:, None]
    wprev = jnp.where(pos != 0, conv_w[0:1], 0.0)
    wnext = jnp.where(pos != row - 1, conv_w[2:3], 0.0)
    consts = [lw["norm2_g"], lw["fourier_out_w"], lw["lru_out_w"], lw["mix_out_w"], lw["ffn_up_w"],
              wprev, conv_w[1:2], wnext, lw["ffn_conv_b"], lw["ffn_down_w"], lw["final_g"]]
    hidden = lw["ffn_down_w"].shape[0]
    return pl.pallas_call(
        functools.partial(_merge_ffn_body, row=row, final_norm=final_norm),
        grid=(b, l // tl),
        in_specs=[tok(d), tok(f.shape[2]), tok(yr.shape[2]), tok(2 * d), tok(d), mod_spec]
        + [_const_spec(a.shape) for a in consts],
        out_specs=tok(d),
        out_shape=jax.ShapeDtypeStruct((b, l, d), F32),
        scratch_shapes=[pltpu.VMEM((tl, hidden), BF16)],
        compiler_params=_params(2),
        name="merge_ffn",
    )(h, f, yr, g01, ycg, mod, *consts)


def _dft_constants(l, group_width, groups):
    half = l // 2
    k = np.arange(half, dtype=np.int64)
    ang = 2.0 * np.pi * ((k[:, None] * k[None, :]) % half).astype(np.float64) / half
    scale = 1.0 / np.sqrt(float(l) * group_width)
    dc = np.cos(ang) * scale
    ds = -np.sin(ang) * scale
    phi = (2.0 * np.pi * k.astype(np.float64) / l)[:, None] * np.ones((1, group_width * groups))
    twiddles = (jnp.asarray(np.cos(phi), F32), jnp.asarray(np.sin(phi), F32))
    c = np.arange(group_width, dtype=np.int64)
    cang = 2.0 * np.pi * ((c[:, None] * c[None, :]) % group_width).astype(np.float64) / group_width
    eye = np.eye(max(1, min(groups, MXU_TILE // group_width)))
    chan = np.concatenate([np.kron(eye, np.cos(cang)), np.kron(eye, np.sin(cang))], axis=1)
    return (jnp.asarray(dc, BF16), jnp.asarray(ds, BF16)) + twiddles, jnp.asarray(chan, BF16)


def _block_diag(w):
    h, i, j = w.shape
    eye = jnp.eye(h, dtype=w.dtype)
    return (eye[:, None, :, None] * w[:, :, None, :]).reshape(h * i, h * j)


def kernel(x, c, ctx, c_ctx, ada_w, ada_b, norm1_g, norm2_g, in_w, in_b, fourier_out_w, lru_conv_w, lru_conv_b, lru_wa, lru_ba, lru_wx, lru_bx, lru_lam, lru_out_w, conf_conv_w, conf_conv_b, conf_ln_g, conf_ln_b, conf_out_w, mix_out_w, ffn_up_w, ffn_conv_w, ffn_conv_b, ffn_down_w, final_g):
    batch, seq, d = x.shape
    ctx_len = ctx.shape[1]
    depth = ada_w.shape[0]
    fw = fourier_out_w.shape[1]
    lwid = lru_out_w.shape[1]
    cw = conf_out_w.shape[1]
    assert batch + 1 <= MOD_ROWS and seq % GRID_W == 0
    assert lwid == cw

    cs = jnp.zeros((MOD_ROWS, d), F32).at[:batch].set(c).at[batch].set(c_ctx)
    mod = _ada_call(cs, ada_w, ada_b)

    dft = {}
    for length in (seq, ctx_len):
        dft[length], chan_dft = _dft_constants(length, fw // FOURIER_GROUPS, FOURIER_GROUPS)

    row2 = lambda a: a.reshape(1, -1)
    h_lat, h_ctx = x, ctx
    zero_state = jnp.zeros((batch, 2, lwid), F32)
    for i in range(depth):
        heads_per_tile = MXU_TILE // lru_wa.shape[-1]
        gate_w = jnp.stack([
            jnp.stack([_block_diag(w[t:t + heads_per_tile]) for t in range(0, w.shape[0], heads_per_tile)])
            for w in (lru_wa[i, 0], lru_wx[i, 0], lru_wa[i, 1], lru_wx[i, 1])])
        gate_b = jnp.concatenate([lru_ba[i, 0], lru_bx[i, 0], lru_ba[i, 1], lru_bx[i, 1]])
        lw = dict(fw=fw, lw=lwid, cw=cw,
                  norm1_g=row2(norm1_g[i]), norm2_g=row2(norm2_g[i]), in_w=in_w[i].astype(BF16), in_b=row2(in_b[i]),
                  chan_dft=chan_dft, lru_conv_w=lru_conv_w[i], lru_conv_b=row2(lru_conv_b[i]),
                  conf_conv_w=conf_conv_w[i], conf_conv_b=row2(conf_conv_b[i]),
                  conf_ln_g=row2(conf_ln_g[i]), conf_ln_b=row2(conf_ln_b[i]), conf_out_w=conf_out_w[i].astype(BF16),
                  lru_gate_w=(0.5 * gate_w).astype(BF16), lru_gate_b=row2(0.5 * gate_b), lru_lam=lru_lam[i],
                  fourier_out_w=fourier_out_w[i].astype(BF16), lru_out_w=lru_out_w[i].astype(BF16),
                  mix_out_w=mix_out_w[i].astype(BF16), ffn_up_w=ffn_up_w[i].astype(BF16),
                  ffn_conv_w=ffn_conv_w[i], ffn_conv_b=row2(ffn_conv_b[i]), ffn_down_w=ffn_down_w[i].astype(BF16),
                  final_g=row2(final_g))
        last = i == depth - 1
        mod_lat = mod[i, :batch].reshape(batch, 1, -1)
        mod_ctx = mod[i, batch:batch + 1].reshape(1, 1, -1)

        flat = lambda a: a.reshape(1, batch * ctx_len, a.shape[-1])
        per_seq = lambda a: a.reshape(batch, ctx_len, a.shape[-1])
        if last:
            xc_c = _lru_input_call(flat(h_ctx), mod_ctx, lw, ctx_len, True)
            _, state_c = _lru_scan_call(per_seq(xc_c), None, zero_state, lw)
        else:
            zcs_c, xc_c, ug_c, g01_c, ycg_c = _mixer_local_call(flat(h_ctx), mod_ctx, lw, ctx_len, True)
            yr_c, state_c = _lru_scan_call(per_seq(xc_c), per_seq(ug_c), zero_state, lw)
            f_c = _fourier_seq_call(per_seq(zcs_c), *dft[ctx_len])
            h_ctx = per_seq(_merge_ffn_call(flat(h_ctx), flat(f_c), flat(yr_c), g01_c, ycg_c, mod_ctx, lw,
                                            ctx_len, True, False))

        zcs, xc, ug, g01, ycg = _mixer_local_call(h_lat, mod_lat, lw, GRID_W, False)
        f = _fourier_seq_call(zcs, *dft[seq])
        yr, _ = _lru_scan_call(xc, ug, state_c, lw)
        h_lat = _merge_ffn_call(h_lat, f, yr, g01, ycg, mod_lat, lw, GRID_W, False, last)
    return h_lat
```

```python
import functools

import numpy as np
import jax
import jax.numpy as jnp
from jax import lax
from jax.experimental import pallas as pl
from jax.experimental.pallas import tpu as pltpu

F32 = jnp.float32
BF16 = jnp.bfloat16

GRID_W = 64
FOURIER_GROUPS = 4
LRU_HEADS = 8
LRU_C = 8.0
N_BRANCH = 3
EPS = 1e-6
LN_EPS = 1e-5

VMEM_LIMIT_BYTES = 56 * 1024 * 1024
SUBLANES = 8
CONV_CHUNK = 64
PAD_ROWS = 16
TOKEN_TILE = 512
HIDDEN_CHUNK = 256
GATE_CHUNK = 256
LANES = 128
MXU_TILE = 256
MOD_ROWS = 24


def _const_spec(shape):
    nd = len(shape)
    return pl.BlockSpec(shape, lambda *_: (0,) * nd, pipeline_mode=pl.Buffered(1))


def _params(n_grid):
    return pltpu.CompilerParams(dimension_semantics=("arbitrary",) * n_grid,
                                vmem_limit_bytes=VMEM_LIMIT_BYTES)


def _dot(a, b):
    return jnp.dot(a, b, preferred_element_type=F32)


def _gelu_tanh(x):
    c1 = float(np.sqrt(2.0 / np.pi))
    half = 0.5 * x
    return half + half * jnp.tanh(x * (c1 + (c1 * 0.044715) * (x * x)))


def _rms(x, g):
    return x * lax.rsqrt(jnp.mean(x * x, axis=-1, keepdims=True) + EPS) * g


def _ada_body(c_ref, w_ref, b_ref, o_ref):
    c = c_ref[...]
    s = c * jax.nn.sigmoid(c)
    o_ref[0] = jnp.dot(s, w_ref[0], preferred_element_type=F32,
                       precision=lax.Precision.HIGHEST) + b_ref[0]


def _ada_call(cs, ada_w, ada_b):
    depth, d, n = ada_w.shape
    nblk = 1536
    return pl.pallas_call(
        _ada_body,
        grid=(depth, n // nblk),
        in_specs=[pl.BlockSpec((MOD_ROWS, d), lambda l, j: (0, 0)),
                  pl.BlockSpec((1, d, nblk), lambda l, j: (l, 0, j)),
                  pl.BlockSpec((1, 1, nblk), lambda l, j: (l, 0, j))],
        out_specs=pl.BlockSpec((1, MOD_ROWS, nblk), lambda l, j: (l, 0, j)),
        out_shape=jax.ShapeDtypeStruct((depth, MOD_ROWS, n), F32),
        compiler_params=_params(2),
        name="adaln_mod",
    )(cs, ada_w, ada_b.reshape(depth, 1, n))


def _fill_padded(pad_ref, val, row):
    tl, c = val.shape
    pad_ref[:, PAD_ROWS:PAD_ROWS + row, :] = val.reshape(tl // row, row, c)


def _row_conv_short(pad_ref, val, w_ref, left, row):
    _fill_padded(pad_ref, val, row)
    pieces = []
    for ri in range(val.shape[0] // row):
        for p in range(row // CONV_CHUNK):
            base = PAD_ROWS + p * CONV_CHUNK - left
            acc = pad_ref[ri, base:base + CONV_CHUNK, :] * w_ref[0:1, :]
            for j in range(1, w_ref.shape[0]):
                acc = acc + pad_ref[ri, base + j:base + j + CONV_CHUNK, :] * w_ref[j:j + 1, :]
            pieces.append(acc)
    return jnp.concatenate(pieces, axis=0)


def _tap_offset(left, j, p):
    return divmod(PAD_ROWS - left + j + p * CONV_CHUNK, SUBLANES)


def _build_phase_copies(pad_ref, shift_ref, ri, n_taps, left, lanes=slice(None)):
    span = shift_ref.shape[1]
    for s in sorted({_tap_offset(left, j, 0)[1] for j in range(n_taps)} - {0}):
        shift_ref[s - 1, :, lanes] = pad_ref[ri, s:s + span, lanes]


def _row_conv_wide_chunk(pad_ref, shift_ref, w_ref, left, ri, p, lanes=slice(None)):
    acc = None
    for j in range(w_ref.shape[0]):
        q, s = _tap_offset(left, j, p)
        rows = slice(q * SUBLANES, q * SUBLANES + CONV_CHUNK)
        src = pad_ref[ri, rows, lanes] if s == 0 else shift_ref[s - 1, rows, lanes]
        term = src * w_ref[j:j + 1, lanes]
        acc = term if acc is None else acc + term
    return acc


def _zero_margins(pad_ref, row):
    zeros = jnp.zeros((pad_ref.shape[0], PAD_ROWS, pad_ref.shape[2]), F32)
    pad_ref[:, 0:PAD_ROWS, :] = zeros
    pad_ref[:, PAD_ROWS + row:, :] = zeros


def _normed_input(h_ref, mod_ref, g_ref, d):
    mod = mod_ref[0]
    n = _rms(h_ref[0], g_ref[...]) * (1.0 + mod[:, d:2 * d]) + mod[:, 0:d]
    return n.astype(BF16)


def _mixer_local_body(h_ref, mod_ref, g_ref, w_ref, b_ref, cs_ref, lcw_ref, lcb_ref,
                      ccw_ref, ccb_ref, lng_ref, lnb_ref, cow_ref,
                      zcs_ref, xc_ref, ug_ref, g01_ref, ycg_ref, pad_ref, shift_ref, conv_ref, gate_ref,
                      *, row, dims):
    d, fw, lw, cw = dims
    off_lx = fw
    off_lg = off_lx + lw
    off_c = off_lg + lw
    off_g = off_c + 2 * cw
    tl = h_ref.shape[1]

    _zero_margins(pad_ref, row)
    nb = _normed_input(h_ref, mod_ref, g_ref, d)

    def proj(lo, hi):
        return _dot(nb, w_ref[:, lo:hi]) + b_ref[:, lo:hi]

    ux = proj(off_lx, off_lg)
    units = [(ri, p) for ri in range(tl // row) for p in range(row // CONV_CHUNK)]
    n_taps = lcw_ref.shape[0]
    for c in range(lw // LANES):
        lanes = slice(c * LANES, (c + 1) * LANES)
        pad_ref[:, PAD_ROWS:PAD_ROWS + row, lanes] = ux[:, lanes].reshape(tl // row, row, LANES)
        for u, (ri, p) in enumerate(units):
            if p == 0:
                _build_phase_copies(pad_ref, shift_ref, ri, n_taps, n_taps // 2, lanes)
            xc = _row_conv_wide_chunk(pad_ref, shift_ref, lcw_ref, n_taps // 2, ri, p, lanes) + lcb_ref[:, lanes]
            xc_ref[0, u * CONV_CHUNK:(u + 1) * CONV_CHUNK, lanes] = xc.astype(xc_ref.dtype)

    def fourier_job():
        uf = proj(0, fw).astype(BF16)
        kt = cs_ref.shape[0]
        for t in range(fw // kt):
            z = _dot(uf[:, t * kt:(t + 1) * kt], cs_ref[...]).astype(zcs_ref.dtype)
            zcs_ref[0, :, t * kt:(t + 1) * kt] = z[:, 0:kt]
            zcs_ref[0, :, fw + t * kt:fw + (t + 1) * kt] = z[:, kt:2 * kt]

    def gate_branch_job():
        ug_ref[0] = proj(off_lg, off_c).astype(ug_ref.dtype)

    def gate_job(lo):
        s = jax.nn.sigmoid(proj(off_g + lo, off_g + lo + GATE_CHUNK))
        if lo < 2 * d:
            g01_ref[0, :, lo:lo + GATE_CHUNK] = s.astype(g01_ref.dtype)
        else:
            gate_ref[:, lo - 2 * d:lo - 2 * d + GATE_CHUNK] = s

    def conformer_job(c):
        lanes = slice(c * LANES, (c + 1) * LANES)
        u = proj(off_c + 2 * c * LANES, off_c + 2 * (c + 1) * LANES)
        glu = u[:, :LANES] * jax.nn.sigmoid(u[:, LANES:])
        pad_ref[:, PAD_ROWS:PAD_ROWS + row, lanes] = glu.reshape(tl // row, row, LANES)
        n_taps = ccw_ref.shape[0]
        for ui, (ri, p) in enumerate(units):
            if p == 0:
                _build_phase_copies(pad_ref, shift_ref, ri, n_taps, n_taps // 2, lanes)
            v = _row_conv_wide_chunk(pad_ref, shift_ref, ccw_ref, n_taps // 2, ri, p, lanes) + ccb_ref[:, lanes]
            conv_ref[ui * CONV_CHUNK:(ui + 1) * CONV_CHUNK, lanes] = v

    fourier_job()
    gate_branch_job()
    gate_los = list(range(0, N_BRANCH * d, GATE_CHUNK))
    n_slices = cw // LANES
    per_slice = -(-len(gate_los) // n_slices)
    for c in range(n_slices):
        conformer_job(c)
        for lo in gate_los[c * per_slice:(c + 1) * per_slice]:
            gate_job(lo)

    v = conv_ref[...]
    mu = jnp.mean(v, axis=-1, keepdims=True)
    vc = v - mu
    var = jnp.mean(vc * vc, axis=-1, keepdims=True)
    v = vc * lax.rsqrt(var + LN_EPS) * lng_ref[...] + lnb_ref[...]
    yc = _dot((v * jax.nn.sigmoid(v)).astype(BF16), cow_ref[...])
    ycg_ref[0] = (gate_ref[...] * yc).astype(ycg_ref.dtype)


def _lru_input_body(h_ref, mod_ref, g_ref, w_ref, b_ref, lcw_ref, lcb_ref, xc_ref, pad_ref, *, row, d):
    _zero_margins(pad_ref, row)
    nb = _normed_input(h_ref, mod_ref, g_ref, d)
    ux = _dot(nb, w_ref[...]) + b_ref[...]
    xc = _row_conv_short(pad_ref, ux, lcw_ref, lcw_ref.shape[0] // 2, row) + lcb_ref[...]
    xc_ref[0] = xc.astype(xc_ref.dtype)


def _token_specs(l, shared_mod, mod):
    tl = min(TOKEN_TILE, l)
    mod_map = (lambda bi, ti: (0, 0, 0)) if shared_mod else (lambda bi, ti: (bi, 0, 0))
    tok = lambda width: pl.BlockSpec((1, tl, width), lambda bi, ti: (bi, ti, 0))
    return tl, tok, pl.BlockSpec((1, 1, mod.shape[2]), mod_map)


def _mixer_local_call(h, mod, lw, row, shared_mod):
    b, l, d = h.shape
    fw, lwid, cw = lw["fw"], lw["lw"], lw["cw"]
    tl, tok, mod_spec = _token_specs(l, shared_mod, mod)
    body = functools.partial(_mixer_local_body, row=row, dims=(d, fw, lwid, cw))
    consts = [lw["norm1_g"], lw["in_w"], lw["in_b"], lw["chan_dft"], lw["lru_conv_w"], lw["lru_conv_b"],
              lw["conf_conv_w"], lw["conf_conv_b"], lw["conf_ln_g"], lw["conf_ln_b"], lw["conf_out_w"]]
    widths = [2 * fw, lwid, lwid, 2 * d, d]
    return pl.pallas_call(
        body,
        grid=(b, l // tl),
        in_specs=[tok(d), mod_spec] + [_const_spec(a.shape) for a in consts],
        out_specs=[tok(w) for w in widths],
        out_shape=[jax.ShapeDtypeStruct((b, l, w), BF16) for w in widths],
        scratch_shapes=[pltpu.VMEM((tl // row, row + 2 * PAD_ROWS, max(lwid, cw)), F32),
                        pltpu.VMEM((SUBLANES - 1, row + 2 * PAD_ROWS - SUBLANES, cw), F32),
                        pltpu.VMEM((tl, cw), F32),
                        pltpu.VMEM((tl, d), F32)],
        compiler_params=_params(2),
        name="mixer_local",
    )(h, mod, *consts)


def _lru_input_call(h, mod, lw, row, shared_mod):
    b, l, d = h.shape
    fw, lwid = lw["fw"], lw["lw"]
    tl, tok, mod_spec = _token_specs(l, shared_mod, mod)
    consts = [lw["norm1_g"], lw["in_w"][:, fw:fw + lwid], lw["in_b"][:, fw:fw + lwid],
              lw["lru_conv_w"], lw["lru_conv_b"]]
    return pl.pallas_call(
        functools.partial(_lru_input_body, row=row, d=d),
        grid=(b, l // tl),
        in_specs=[tok(d), mod_spec] + [_const_spec(a.shape) for a in consts],
        out_specs=tok(lwid),
        out_shape=jax.ShapeDtypeStruct((b, l, lwid), BF16),
        scratch_shapes=[pltpu.VMEM((tl // row, row + 2 * PAD_ROWS, lwid), F32)],
        compiler_params=_params(2),
        name="lru_input",
    )(h, mod, *consts)


def _segment_carries(h_fin, p_fin, h0, reverse):
    rows = [None] * SUBLANES
    c = h0
    for s in (range(SUBLANES - 1, -1, -1) if reverse else range(SUBLANES)):
        rows[s] = c
        c = h_fin[:, s:s + 1, :] + p_fin[:, s:s + 1, :] * c
    return jnp.concatenate(rows, axis=1), c


def _lru_scan_body(*refs, chunk, with_output):
    if with_output:
        xc_ref, ug_ref, h0_ref, wg_ref, bg_ref, lam_ref, yr_ref, hfin_ref, af_ref, bf_ref, ab_ref, bb_ref = refs
    else:
        xc_ref, h0_ref, wg_ref, bg_ref, lam_ref, hfin_ref, af_ref, bf_ref, ab_ref, bb_ref = refs
    l, w = xc_ref.shape[1], xc_ref.shape[2]
    groups = w // LANES
    seg = l // SUBLANES
    piece = min(chunk, seg)
    neg = -lam_ref[...]
    softplus = jnp.maximum(neg, 0.0) + jnp.log1p(jnp.exp(-jnp.abs(neg)))
    half_rate = (-0.5 * LRU_C * np.log2(np.e)) * softplus

    def seg_rows(t0):
        s, k0 = divmod(t0, seg)
        return pl.ds(SUBLANES * k0 + s, piece, stride=SUBLANES)

    def scatter(ref, val, t0):
        for q in range(val.shape[0] // piece):
            for g in range(groups):
                ref[g, seg_rows(t0 + q * piece), :] = val[q * piece:(q + 1) * piece, g * LANES:(g + 1) * LANES]

    for i in range(l // chunk):
        rows = slice(i * chunk, (i + 1) * chunk)
        xb = xc_ref[0, rows, :]
        half_x = 0.5 * xb.astype(F32)

        def gate_tanh(gi):
            kt = wg_ref.shape[2]
            pre = [_dot(xb[:, t * kt:(t + 1) * kt], wg_ref[gi, t]) for t in range(w // kt)]
            return jnp.tanh(jnp.concatenate(pre, axis=1) + bg_ref[:, gi * w:(gi + 1) * w])

        for di, (a_ref, b_ref) in enumerate(((af_ref, bf_ref), (ab_ref, bb_ref))):
            rate = half_rate[di:di + 1, :]
            a = jnp.exp2(gate_tanh(2 * di) * rate + rate)
            gated_x = (gate_tanh(2 * di + 1) + 1.0) * half_x
            scatter(a_ref, a, i * chunk)
            y = 1.0 - a * a
            root = jnp.where(y > 0.0, y * lax.rsqrt(y), 0.0)
            scatter(b_ref, root * gated_x, i * chunk)

    def step(k, carry):
        hf, pf, hb, pb = carry
        rf = pl.ds(pl.multiple_of(k * SUBLANES, SUBLANES), SUBLANES)
        rb = pl.ds(pl.multiple_of((seg - 1 - k) * SUBLANES, SUBLANES), SUBLANES)
        a = af_ref[:, rf, :]
        hf = a * hf + bf_ref[:, rf, :]
        pf = a * pf
        bf_ref[:, rf, :] = hf
        af_ref[:, rf, :] = pf
        a = ab_ref[:, rb, :]
        hb = a * hb + bb_ref[:, rb, :]
        pb = a * pb
        bb_ref[:, rb, :] = hb
        ab_ref[:, rb, :] = pb
        return hf, pf, hb, pb

    zeros = jnp.zeros((groups, SUBLANES, LANES), F32)
    ones = jnp.ones((groups, SUBLANES, LANES), F32)
    hf, pf, hb, pb = lax.fori_loop(0, seg, step, (zeros, ones, zeros, ones), unroll=8)

    h0 = h0_ref[0]
    split = lambda row: jnp.stack([row[:, g * LANES:(g + 1) * LANES] for g in range(groups)])
    cf, end_f = _segment_carries(hf, pf, split(h0[0:1, :]), False)
    cb, end_b = _segment_carries(hb, pb, split(h0[1:2, :]), True)
    for g in range(groups):
        hfin_ref[0, 0:1, g * LANES:(g + 1) * LANES] = end_f[g]
        hfin_ref[0, 1:2, g * LANES:(g + 1) * LANES] = end_b[g]

    if with_output:
        for t0 in range(0, l, piece):
            s = t0 // seg
            for g in range(groups):
                lanes = slice(g * LANES, (g + 1) * LANES)
                h = (bf_ref[g, seg_rows(t0), :] + af_ref[g, seg_rows(t0), :] * cf[g, s:s + 1, :]
                     + bb_ref[g, seg_rows(t0), :] + ab_ref[g, seg_rows(t0), :] * cb[g, s:s + 1, :])
                y = h * _gelu_tanh(ug_ref[0, t0:t0 + piece, lanes].astype(F32))
                yr_ref[0, t0:t0 + piece, lanes] = y.astype(yr_ref.dtype)


def _lru_scan_call(xc, ug, h0, lw):
    b, l, w = xc.shape
    seq = pl.BlockSpec((1, l, w), lambda bi: (bi, 0, 0))
    st = pl.BlockSpec((1, 2, w), lambda bi: (bi, 0, 0))
    consts = [lw["lru_gate_w"], lw["lru_gate_b"], lw["lru_lam"]]
    with_output = ug is not None
    st_shape = jax.ShapeDtypeStruct((b, 2, w), F32)
    out = pl.pallas_call(
        functools.partial(_lru_scan_body, chunk=min(256, l), with_output=with_output),
        grid=(b,),
        in_specs=[seq] * (2 if with_output else 1) + [st] + [_const_spec(a.shape) for a in consts],
        out_specs=[seq, st] if with_output else st,
        out_shape=[jax.ShapeDtypeStruct((b, l, w), BF16), st_shape] if with_output else st_shape,
        scratch_shapes=[pltpu.VMEM((w // LANES, l, LANES), F32)] * 4,
        compiler_params=_params(1),
        name="lru_scan",
    )(*([xc, ug] if with_output else [xc]), h0, *consts)
    return out if with_output else (None, out)


def _fourier_seq_body(z_ref, dc_ref, ds_ref, twc_ref, tws_ref, o_ref, fold_ref, il_ref, *, chunk):
    l, fw = o_ref.shape[1], o_ref.shape[2]
    half = l // 2
    for i in range(half // chunk):
        rows = slice(i * chunk, (i + 1) * chunk)
        lo = z_ref[0, rows, :].astype(F32)
        hi = z_ref[0, half + i * chunk:half + (i + 1) * chunk, :].astype(F32)
        tot, dif = lo + hi, lo - hi
        dre, dim = dif[:, 0:fw], dif[:, fw:2 * fw]
        cos, sin = twc_ref[rows, :], tws_ref[rows, :]
        fold_ref[0, rows, :] = tot[:, 0:fw].astype(fold_ref.dtype)
        fold_ref[1, rows, :] = tot[:, fw:2 * fw].astype(fold_ref.dtype)
        fold_ref[2, rows, :] = (dre * cos - dim * sin).astype(fold_ref.dtype)
        fold_ref[3, rows, :] = (dre * sin + dim * cos).astype(fold_ref.dtype)
    for i in range(half // chunk):
        rows = slice(i * chunk, (i + 1) * chunk)
        even = _dot(dc_ref[rows, :], fold_ref[0]) + _dot(ds_ref[rows, :], fold_ref[1])
        odd = _dot(dc_ref[rows, :], fold_ref[2]) + _dot(ds_ref[rows, :], fold_ref[3])
        for g in range(fw // LANES):
            lanes = slice(g * LANES, (g + 1) * LANES)
            il_ref[g, pl.ds(2 * i * chunk, chunk, stride=2), :] = even[:, lanes]
            il_ref[g, pl.ds(2 * i * chunk + 1, chunk, stride=2), :] = odd[:, lanes]
            out_rows = slice(2 * i * chunk, 2 * (i + 1) * chunk)
            o_ref[0, out_rows, lanes] = il_ref[g, out_rows, :].astype(o_ref.dtype)


def _fourier_seq_call(zcs, dc, ds, twc, tws):
    b, l, w2 = zcs.shape
    fw = w2 // 2
    consts = [dc, ds, twc, tws]
    return pl.pallas_call(
        functools.partial(_fourier_seq_body, chunk=min(256, l // 2)),
        grid=(b,),
        in_specs=[pl.BlockSpec((1, l, w2), lambda bi: (bi, 0, 0))] + [_const_spec(a.shape) for a in consts],
        out_specs=pl.BlockSpec((1, l, fw), lambda bi: (bi, 0, 0)),
        out_shape=jax.ShapeDtypeStruct((b, l, fw), BF16),
        scratch_shapes=[pltpu.VMEM((4, l // 2, fw), BF16), pltpu.VMEM((fw // LANES, l, LANES), F32)],
        compiler_params=_params(1),
        name="fourier_seq",
    )(zcs, *consts)


def _merge_ffn_body(h_ref, f_ref, yr_ref, g01_ref, ycg_ref, mod_ref, g2_ref, fow_ref, low_ref, mow_ref,
                    upw_ref, wprev_ref, wmid_ref, wnext_ref, fcb_ref, dww_ref, fin_ref, o_ref, act_ref,
                    *, row, final_norm):
    tl, d = h_ref.shape[1], h_ref.shape[2]
    hidden = dww_ref.shape[0]
    hc = HIDDEN_CHUNK
    r = tl // row
    mod = mod_ref[0]
    gate1, sh2, sc2, gate2 = (mod[:, 2 * d:3 * d], mod[:, 3 * d:4 * d], mod[:, 4 * d:5 * d], mod[:, 5 * d:6 * d])

    yf = _dot(f_ref[0], fow_ref[...])
    yr = _dot(yr_ref[0], low_ref[...])
    merged = g01_ref[0, :, 0:d] * yf + g01_ref[0, :, d:2 * d] * yr + ycg_ref[0]
    h1 = h_ref[0] + gate1 * _dot(merged.astype(BF16), mow_ref[...])

    nb = (_rms(h1, g2_ref[...]) * (1.0 + sc2) + sh2).astype(BF16)

    def conv3(u, lo):
        cols = slice(lo, lo + hc)
        prev = pltpu.roll(u, 1, 0).reshape(r, row, hc) * wprev_ref[:, cols][None]
        nxt = pltpu.roll(u, tl - 1, 0).reshape(r, row, hc) * wnext_ref[:, cols][None]
        out = u.reshape(r, row, hc) * wmid_ref[:, cols][None] + prev + nxt + fcb_ref[:, cols][None]
        return out.reshape(tl, hc)

    for j in range(hidden // hc):
        lo = j * hc
        v = conv3(_dot(nb, upw_ref[:, lo:lo + hc]), lo)
        g = conv3(_dot(nb, upw_ref[:, hidden + lo:hidden + lo + hc]), hidden + lo)
        act_ref[:, lo:lo + hc] = (g * jax.nn.sigmoid(g) * v).astype(act_ref.dtype)
    h2 = h1 + gate2 * _dot(act_ref[...], dww_ref[...])
    if final_norm:
        h2 = _rms(h2, fin_ref[...])
    o_ref[0] = h2


def _merge_ffn_call(h, f, yr, g01, ycg, mod, lw, row, shared_mod, final_norm):
    b, l, d = h.shape
    tl, tok, mod_spec = _token_specs(l, shared_mod, mod)
    conv_w = lw["ffn_conv_w"]
    pos = jnp.arange(row)[:, None]
    wprev = jnp.where(pos != 0, conv_w[0:1], 0.0)
    wnext = jnp.where(pos != row - 1, conv_w[2:3], 0.0)
    consts = [lw["norm2_g"], lw["fourier_out_w"], lw["lru_out_w"], lw["mix_out_w"], lw["ffn_up_w"],
              wprev, conv_w[1:2], wnext, lw["ffn_conv_b"], lw["ffn_down_w"], lw["final_g"]]
    hidden = lw["ffn_down_w"].shape[0]
    return pl.pallas_call(
        functools.partial(_merge_ffn_body, row=row, final_norm=final_norm),
        grid=(b, l // tl),
        in_specs=[tok(d), tok(f.shape[2]), tok(yr.shape[2]), tok(2 * d), tok(d), mod_spec]
        + [_const_spec(a.shape) for a in consts],
        out_specs=tok(d),
        out_shape=jax.ShapeDtypeStruct((b, l, d), F32),
        scratch_shapes=[pltpu.VMEM((tl, hidden), BF16)],
        compiler_params=_params(2),
        name="merge_ffn",
    )(h, f, yr, g01, ycg, mod, *consts)


def _dft_constants(l, group_width, groups):
    half = l // 2
    k = np.arange(half, dtype=np.int64)
    ang = 2.0 * np.pi * ((k[:, None] * k[None, :]) % half).astype(np.float64) / half
    scale = 1.0 / np.sqrt(float(l) * group_width)
    dc = np.cos(ang) * scale
    ds = -np.sin(ang) * scale
    phi = (2.0 * np.pi * k.astype(np.float64) / l)[:, None] * np.ones((1, group_width * groups))
    twiddles = (jnp.asarray(np.cos(phi), F32), jnp.asarray(np.sin(phi), F32))
    c = np.arange(group_width, dtype=np.int64)
    cang = 2.0 * np.pi * ((c[:, None] * c[None, :]) % group_width).astype(np.float64) / group_width
    eye = np.eye(max(1, min(groups, MXU_TILE // group_width)))
    chan = np.concatenate([np.kron(eye, np.cos(cang)), np.kron(eye, np.sin(cang))], axis=1)
    return (jnp.asarray(dc, BF16), jnp.asarray(ds, BF16)) + twiddles, jnp.asarray(chan, BF16)


def _block_diag(w):
    h, i, j = w.shape
    eye = jnp.eye(h, dtype=w.dtype)
    return (eye[:, None, :, None] * w[:, :, None, :]).reshape(h * i, h * j)


def kernel(x, c, ctx, c_ctx, ada_w, ada_b, norm1_g, norm2_g, in_w, in_b, fourier_out_w, lru_conv_w, lru_conv_b, lru_wa, lru_ba, lru_wx, lru_bx, lru_lam, lru_out_w, conf_conv_w, conf_conv_b, conf_ln_g, conf_ln_b, conf_out_w, mix_out_w, ffn_up_w, ffn_conv_w, ffn_conv_b, ffn_down_w, final_g):
    batch, seq, d = x.shape
    ctx_len = ctx.shape[1]
    depth = ada_w.shape[0]
    fw = fourier_out_w.shape[1]
    lwid = lru_out_w.shape[1]
    cw = conf_out_w.shape[1]
    assert batch + 1 <= MOD_ROWS and seq % GRID_W == 0
    assert lwid == cw

    cs = jnp.zeros((MOD_ROWS, d), F32).at[:batch].set(c).at[batch].set(c_ctx)
    mod = _ada_call(cs, ada_w, ada_b)

    dft = {}
    for length in (seq, ctx_len):
        dft[length], chan_dft = _dft_constants(length, fw // FOURIER_GROUPS, FOURIER_GROUPS)

    off_c = fw + 2 * lwid
    slices = np.arange(cw).reshape(cw // LANES, LANES)
    col_order = np.concatenate([np.arange(off_c)] + [np.concatenate([off_c + s, off_c + cw + s]) for s in slices]
                               + [np.arange(off_c + 2 * cw, in_w.shape[2])])

    row2 = lambda a: a.reshape(1, -1)
    h_lat, h_ctx = x, ctx
    zero_state = jnp.zeros((batch, 2, lwid), F32)
    for i in range(depth):
        heads_per_tile = MXU_TILE // lru_wa.shape[-1]
        gate_w = jnp.stack([
            jnp.stack([_block_diag(w[t:t + heads_per_tile]) for t in range(0, w.shape[0], heads_per_tile)])
            for w in (lru_wa[i, 0], lru_wx[i, 0], lru_wa[i, 1], lru_wx[i, 1])])
        gate_b = jnp.concatenate([lru_ba[i, 0], lru_bx[i, 0], lru_ba[i, 1], lru_bx[i, 1]])
        lw = dict(fw=fw, lw=lwid, cw=cw,
                  norm1_g=row2(norm1_g[i]), norm2_g=row2(norm2_g[i]),
                  in_w=in_w[i][:, col_order].astype(BF16), in_b=row2(in_b[i][col_order]),
                  chan_dft=chan_dft, lru_conv_w=lru_conv_w[i], lru_conv_b=row2(lru_conv_b[i]),
                  conf_conv_w=conf_conv_w[i], conf_conv_b=row2(conf_conv_b[i]),
                  conf_ln_g=row2(conf_ln_g[i]), conf_ln_b=row2(conf_ln_b[i]), conf_out_w=conf_out_w[i].astype(BF16),
                  lru_gate_w=(0.5 * gate_w).astype(BF16), lru_gate_b=row2(0.5 * gate_b), lru_lam=lru_lam[i],
                  fourier_out_w=fourier_out_w[i].astype(BF16), lru_out_w=lru_out_w[i].astype(BF16),
                  mix_out_w=mix_out_w[i].astype(BF16), ffn_up_w=ffn_up_w[i].astype(BF16),
                  ffn_conv_w=ffn_conv_w[i], ffn_conv_b=row2(ffn_conv_b[i]), ffn_down_w=ffn_down_w[i].astype(BF16),
                  final_g=row2(final_g))
        last = i == depth - 1
        mod_lat = mod[i, :batch].reshape(batch, 1, -1)
        mod_ctx = mod[i, batch:batch + 1].reshape(1, 1, -1)

        flat = lambda a: a.reshape(1, batch * ctx_len, a.shape[-1])
        per_seq = lambda a: a.reshape(batch, ctx_len, a.shape[-1])
        if last:
            xc_c = _lru_input_call(flat(h_ctx), mod_ctx, lw, ctx_len, True)
            _, state_c = _lru_scan_call(per_seq(xc_c), None, zero_state, lw)
        else:
            zcs_c, xc_c, ug_c, g01_c, ycg_c = _mixer_local_call(flat(h_ctx), mod_ctx, lw, ctx_len, True)
            yr_c, state_c = _lru_scan_call(per_seq(xc_c), per_seq(ug_c), zero_state, lw)
            f_c = _fourier_seq_call(per_seq(zcs_c), *dft[ctx_len])
            h_ctx = per_seq(_merge_ffn_call(flat(h_ctx), flat(f_c), flat(yr_c), g01_c, ycg_c, mod_ctx, lw,
                                            ctx_len, True, False))

        zcs, xc, ug, g01, ycg = _mixer_local_call(h_lat, mod_lat, lw, GRID_W, False)
        f = _fourier_seq_call(zcs, *dft[seq])
        yr, _ = _lru_scan_call(xc, ug, state_c, lw)
        h_lat = _merge_ffn_call(h_lat, f, yr, g01, ycg, mod_lat, lw, GRID_W, False, last)
    return h_lat
```

```python
import functools

import numpy as np
import jax
import jax.numpy as jnp
from jax import lax
from jax.experimental import pallas as pl
from jax.experimental.pallas import tpu as pltpu

F32 = jnp.float32
BF16 = jnp.bfloat16

GRID_W = 64
FOURIER_GROUPS = 4
LRU_HEADS = 8
LRU_C = 8.0
N_BRANCH = 3
EPS = 1e-6
LN_EPS = 1e-5

VMEM_LIMIT_BYTES = 56 * 1024 * 1024
SUBLANES = 8
CONV_CHUNK = 64
PAD_ROWS = 16
TOKEN_TILE = 512
HIDDEN_CHUNK = 256
GATE_CHUNK = 256
LANES = 128
MXU_TILE = 256
MOD_ROWS = 24


def _const_spec(shape):
    nd = len(shape)
    return pl.BlockSpec(shape, lambda *_: (0,) * nd, pipeline_mode=pl.Buffered(1))


def _params(n_grid):
    return pltpu.CompilerParams(dimension_semantics=("arbitrary",) * n_grid,
                                vmem_limit_bytes=VMEM_LIMIT_BYTES)


def _dot(a, b):
    return jnp.dot(a, b, preferred_element_type=F32)


def _gelu_tanh(x):
    c1 = float(np.sqrt(2.0 / np.pi))
    half = 0.5 * x
    return half + half * jnp.tanh(x * (c1 + (c1 * 0.044715) * (x * x)))


def _rms(x, g):
    return x * lax.rsqrt(jnp.mean(x * x, axis=-1, keepdims=True) + EPS) * g


def _ada_body(c_ref, w_ref, b_ref, o_ref):
    c = c_ref[...]
    s = c * jax.nn.sigmoid(c)
    o_ref[0] = jnp.dot(s, w_ref[0], preferred_element_type=F32,
                       precision=lax.Precision.HIGHEST) + b_ref[0]


def _ada_call(cs, ada_w, ada_b):
    depth, d, n = ada_w.shape
    nblk = 1536
    return pl.pallas_call(
        _ada_body,
        grid=(depth, n // nblk),
        in_specs=[pl.BlockSpec((MOD_ROWS, d), lambda l, j: (0, 0)),
                  pl.BlockSpec((1, d, nblk), lambda l, j: (l, 0, j)),
                  pl.BlockSpec((1, 1, nblk), lambda l, j: (l, 0, j))],
        out_specs=pl.BlockSpec((1, MOD_ROWS, nblk), lambda l, j: (l, 0, j)),
        out_shape=jax.ShapeDtypeStruct((depth, MOD_ROWS, n), F32),
        compiler_params=_params(2),
        name="adaln_mod",
    )(cs, ada_w, ada_b.reshape(depth, 1, n))


def _fill_padded(pad_ref, val, row):
    tl, c = val.shape
    pad_ref[:, PAD_ROWS:PAD_ROWS + row, :] = val.reshape(tl // row, row, c)


def _row_conv_short(pad_ref, val, w_ref, left, row):
    _fill_padded(pad_ref, val, row)
    pieces = []
    for ri in range(val.shape[0] // row):
        for p in range(row // CONV_CHUNK):
            base = PAD_ROWS + p * CONV_CHUNK - left
            acc = pad_ref[ri, base:base + CONV_CHUNK, :] * w_ref[0:1, :]
            for j in range(1, w_ref.shape[0]):
                acc = acc + pad_ref[ri, base + j:base + j + CONV_CHUNK, :] * w_ref[j:j + 1, :]
            pieces.append(acc)
    return jnp.concatenate(pieces, axis=0)


def _tap_offset(left, j, p):
    return divmod(PAD_ROWS - left + j + p * CONV_CHUNK, SUBLANES)


def _build_phase_copies(pad_ref, shift_ref, ri, n_taps, left, lanes=slice(None)):
    span = shift_ref.shape[1]
    for s in sorted({_tap_offset(left, j, 0)[1] for j in range(n_taps)} - {0}):
        shift_ref[s - 1, :, lanes] = pad_ref[ri, s:s + span, lanes]


def _row_conv_wide_chunk(pad_ref, shift_ref, w_ref, left, ri, p, lanes=slice(None)):
    acc = None
    for j in range(w_ref.shape[0]):
        q, s = _tap_offset(left, j, p)
        rows = slice(q * SUBLANES, q * SUBLANES + CONV_CHUNK)
        src = pad_ref[ri, rows, lanes] if s == 0 else shift_ref[s - 1, rows, lanes]
        term = src * w_ref[j:j + 1, lanes]
        acc = term if acc is None else acc + term
    return acc


def _zero_margins(pad_ref, row):
    zeros = jnp.zeros((pad_ref.shape[0], PAD_ROWS, pad_ref.shape[2]), F32)
    pad_ref[:, 0:PAD_ROWS, :] = zeros
    pad_ref[:, PAD_ROWS + row:, :] = zeros


def _normed_input(h_ref, mod_ref, g_ref, d):
    mod = mod_ref[0]
    n = _rms(h_ref[0], g_ref[...]) * (1.0 + mod[:, d:2 * d]) + mod[:, 0:d]
    return n.astype(BF16)


def _mixer_local_body(h_ref, mod_ref, g_ref, w_ref, b_ref, cs_ref, lcw_ref, lcb_ref,
                      ccw_ref, ccb_ref, lng_ref, lnb_ref, cow_ref,
                      zcs_ref, xc_ref, ug_ref, g01_ref, ycg_ref, pad_ref, shift_ref, conv_ref, gate_ref,
                      *, row, dims):
    d, fw, lw, cw = dims
    off_lx = fw
    off_lg = off_lx + lw
    off_c = off_lg + lw
    off_g = off_c + 2 * cw
    tl = h_ref.shape[1]

    _zero_margins(pad_ref, row)
    nb = _normed_input(h_ref, mod_ref, g_ref, d)

    def proj(lo, hi):
        return _dot(nb, w_ref[:, lo:hi]) + b_ref[:, lo:hi]

    ux = proj(off_lx, off_lg)
    units = [(ri, p) for ri in range(tl // row) for p in range(row // CONV_CHUNK)]
    n_taps = lcw_ref.shape[0]
    for c in range(lw // LANES):
        lanes = slice(c * LANES, (c + 1) * LANES)
        pad_ref[:, PAD_ROWS:PAD_ROWS + row, lanes] = ux[:, lanes].reshape(tl // row, row, LANES)
        for u, (ri, p) in enumerate(units):
            if p == 0:
                _build_phase_copies(pad_ref, shift_ref, ri, n_taps, n_taps // 2, lanes)
            xc = _row_conv_wide_chunk(pad_ref, shift_ref, lcw_ref, n_taps // 2, ri, p, lanes) + lcb_ref[:, lanes]
            xc_ref[0, u * CONV_CHUNK:(u + 1) * CONV_CHUNK, lanes] = xc.astype(xc_ref.dtype)

    def fourier_job():
        uf = proj(0, fw).astype(BF16)
        kt = cs_ref.shape[0]
        for t in range(fw // kt):
            z = _dot(uf[:, t * kt:(t + 1) * kt], cs_ref[...]).astype(zcs_ref.dtype)
            zcs_ref[0, :, t * kt:(t + 1) * kt] = z[:, 0:kt]
            zcs_ref[0, :, fw + t * kt:fw + (t + 1) * kt] = z[:, kt:2 * kt]

    def gate_branch_job():
        ug_ref[0] = proj(off_lg, off_c).astype(ug_ref.dtype)

    def gate_job(lo):
        s = jax.nn.sigmoid(proj(off_g + lo, off_g + lo + GATE_CHUNK))
        if lo < 2 * d:
            g01_ref[0, :, lo:lo + GATE_CHUNK] = s.astype(g01_ref.dtype)
        else:
            gate_ref[:, lo - 2 * d:lo - 2 * d + GATE_CHUNK] = s

    def conformer_job(c):
        lanes = slice(c * LANES, (c + 1) * LANES)
        u = proj(off_c + 2 * c * LANES, off_c + 2 * (c + 1) * LANES)
        glu = u[:, :LANES] * jax.nn.sigmoid(u[:, LANES:])
        pad_ref[:, PAD_ROWS:PAD_ROWS + row, lanes] = glu.reshape(tl // row, row, LANES)
        n_taps = ccw_ref.shape[0]
        for ui, (ri, p) in enumerate(units):
            if p == 0:
                _build_phase_copies(pad_ref, shift_ref, ri, n_taps, n_taps // 2, lanes)
            v = _row_conv_wide_chunk(pad_ref, shift_ref, ccw_ref, n_taps // 2, ri, p, lanes) + ccb_ref[:, lanes]
            conv_ref[ui * CONV_CHUNK:(ui + 1) * CONV_CHUNK, lanes] = v

    fourier_job()
    gate_branch_job()
    gate_los = list(range(0, N_BRANCH * d, GATE_CHUNK))
    n_slices = cw // LANES
    per_slice = -(-len(gate_los) // n_slices)
    for c in range(n_slices):
        conformer_job(c)
        for lo in gate_los[c * per_slice:(c + 1) * per_slice]:
            gate_job(lo)

    v = conv_ref[...]
    mu = jnp.mean(v, axis=-1, keepdims=True)
    vc = v - mu
    var = jnp.mean(vc * vc, axis=-1, keepdims=True)
    v = vc * lax.rsqrt(var + LN_EPS) * lng_ref[...] + lnb_ref[...]
    yc = _dot((v * jax.nn.sigmoid(v)).astype(BF16), cow_ref[...])
    ycg_ref[0] = (gate_ref[...] * yc).astype(ycg_ref.dtype)


def _lru_input_body(h_ref, mod_ref, g_ref, w_ref, b_ref, lcw_ref, lcb_ref, xc_ref, pad_ref, *, row, d):
    _zero_margins(pad_ref, row)
    nb = _normed_input(h_ref, mod_ref, g_ref, d)
    ux = _dot(nb, w_ref[...]) + b_ref[...]
    xc = _row_conv_short(pad_ref, ux, lcw_ref, lcw_ref.shape[0] // 2, row) + lcb_ref[...]
    xc_ref[0] = xc.astype(xc_ref.dtype)


def _token_specs(l, shared_mod, mod):
    tl = min(TOKEN_TILE, l)
    mod_map = (lambda bi, ti: (0, 0, 0)) if shared_mod else (lambda bi, ti: (bi, 0, 0))
    tok = lambda width: pl.BlockSpec((1, tl, width), lambda bi, ti: (bi, ti, 0))
    return tl, tok, pl.BlockSpec((1, 1, mod.shape[2]), mod_map)


def _mixer_local_call(h, mod, lw, row, shared_mod):
    b, l, d = h.shape
    fw, lwid, cw = lw["fw"], lw["lw"], lw["cw"]
    tl, tok, mod_spec = _token_specs(l, shared_mod, mod)
    body = functools.partial(_mixer_local_body, row=row, dims=(d, fw, lwid, cw))
    consts = [lw["norm1_g"], lw["in_w"], lw["in_b"], lw["chan_dft"], lw["lru_conv_w"], lw["lru_conv_b"],
              lw["conf_conv_w"], lw["conf_conv_b"], lw["conf_ln_g"], lw["conf_ln_b"], lw["conf_out_w"]]
    widths = [2 * fw, lwid, lwid, 2 * d, d]
    return pl.pallas_call(
        body,
        grid=(b, l // tl),
        in_specs=[tok(d), mod_spec] + [_const_spec(a.shape) for a in consts],
        out_specs=[tok(w) for w in widths],
        out_shape=[jax.ShapeDtypeStruct((b, l, w), BF16) for w in widths],
        scratch_shapes=[pltpu.VMEM((tl // row, row + 2 * PAD_ROWS, max(lwid, cw)), F32),
                        pltpu.VMEM((SUBLANES - 1, row + 2 * PAD_ROWS - SUBLANES, cw), F32),
                        pltpu.VMEM((tl, cw), F32),
                        pltpu.VMEM((tl, d), F32)],
        compiler_params=_params(2),
        name="mixer_local",
    )(h, mod, *consts)


def _lru_input_call(h, mod, lw, row, shared_mod):
    b, l, d = h.shape
    fw, lwid = lw["fw"], lw["lw"]
    tl, tok, mod_spec = _token_specs(l, shared_mod, mod)
    consts = [lw["norm1_g"], lw["in_w"][:, fw:fw + lwid], lw["in_b"][:, fw:fw + lwid],
              lw["lru_conv_w"], lw["lru_conv_b"]]
    return pl.pallas_call(
        functools.partial(_lru_input_body, row=row, d=d),
        grid=(b, l // tl),
        in_specs=[tok(d), mod_spec] + [_const_spec(a.shape) for a in consts],
        out_specs=tok(lwid),
        out_shape=jax.ShapeDtypeStruct((b, l, lwid), BF16),
        scratch_shapes=[pltpu.VMEM((tl // row, row + 2 * PAD_ROWS, lwid), F32)],
        compiler_params=_params(2),
        name="lru_input",
    )(h, mod, *consts)


def _segment_carries(h_fin, p_fin, h0, reverse):
    rows = [None] * SUBLANES
    c = h0
    for s in (range(SUBLANES - 1, -1, -1) if reverse else range(SUBLANES)):
        rows[s] = c
        c = h_fin[:, s:s + 1, :] + p_fin[:, s:s + 1, :] * c
    return jnp.concatenate(rows, axis=1), c


def _lru_scan_body(*refs, chunk, with_output):
    if with_output:
        xc_ref, ug_ref, h0_ref, wg_ref, bg_ref, lam_ref, yr_ref, hfin_ref, af_ref, bf_ref, ab_ref, bb_ref = refs
    else:
        xc_ref, h0_ref, wg_ref, bg_ref, lam_ref, hfin_ref, af_ref, bf_ref, ab_ref, bb_ref = refs
    l, w = xc_ref.shape[1], xc_ref.shape[2]
    groups = w // LANES
    seg = l // SUBLANES
    piece = min(chunk, seg)
    neg = -lam_ref[...]
    softplus = jnp.maximum(neg, 0.0) + jnp.log1p(jnp.exp(-jnp.abs(neg)))
    half_rate = (-0.5 * LRU_C * np.log2(np.e)) * softplus

    def seg_rows(t0):
        s, k0 = divmod(t0, seg)
        return pl.ds(SUBLANES * k0 + s, piece, stride=SUBLANES)

    def scatter(ref, val, t0):
        for q in range(val.shape[0] // piece):
            for g in range(groups):
                ref[g, seg_rows(t0 + q * piece), :] = val[q * piece:(q + 1) * piece, g * LANES:(g + 1) * LANES]

    for i in range(l // chunk):
        rows = slice(i * chunk, (i + 1) * chunk)
        xb = xc_ref[0, rows, :]
        half_x = 0.5 * xb.astype(F32)

        def gate_tanh(gi):
            kt = wg_ref.shape[2]
            pre = [_dot(xb[:, t * kt:(t + 1) * kt], wg_ref[gi, t]) for t in range(w // kt)]
            return jnp.tanh(jnp.concatenate(pre, axis=1) + bg_ref[:, gi * w:(gi + 1) * w])

        for di, (a_ref, b_ref) in enumerate(((af_ref, bf_ref), (ab_ref, bb_ref))):
            rate = half_rate[di:di + 1, :]
            a = jnp.exp2(gate_tanh(2 * di) * rate + rate)
            gated_x = (gate_tanh(2 * di + 1) + 1.0) * half_x
            scatter(a_ref, a, i * chunk)
            y = 1.0 - a * a
            root = jnp.where(y > 0.0, y * lax.rsqrt(y), 0.0)
            scatter(b_ref, root * gated_x, i * chunk)

    def step(k, carry):
        hf, pf, hb, pb = carry
        rf = pl.ds(pl.multiple_of(k * SUBLANES, SUBLANES), SUBLANES)
        rb = pl.ds(pl.multiple_of((seg - 1 - k) * SUBLANES, SUBLANES), SUBLANES)
        a = af_ref[:, rf, :]
        hf = a * hf + bf_ref[:, rf, :]
        pf = a * pf
        bf_ref[:, rf, :] = hf
        af_ref[:, rf, :] = pf
        a = ab_ref[:, rb, :]
        hb = a * hb + bb_ref[:, rb, :]
        pb = a * pb
        bb_ref[:, rb, :] = hb
        ab_ref[:, rb, :] = pb
        return hf, pf, hb, pb

    zeros = jnp.zeros((groups, SUBLANES, LANES), F32)
    ones = jnp.ones((groups, SUBLANES, LANES), F32)
    hf, pf, hb, pb = lax.fori_loop(0, seg, step, (zeros, ones, zeros, ones), unroll=8)

    h0 = h0_ref[0]
    split = lambda row: jnp.stack([row[:, g * LANES:(g + 1) * LANES] for g in range(groups)])
    cf, end_f = _segment_carries(hf, pf, split(h0[0:1, :]), False)
    cb, end_b = _segment_carries(hb, pb, split(h0[1:2, :]), True)
    for g in range(groups):
        hfin_ref[0, 0:1, g * LANES:(g + 1) * LANES] = end_f[g]
        hfin_ref[0, 1:2, g * LANES:(g + 1) * LANES] = end_b[g]

    if with_output:
        for t0 in range(0, l, piece):
            s = t0 // seg
            for g in range(groups):
                lanes = slice(g * LANES, (g + 1) * LANES)
                h = (bf_ref[g, seg_rows(t0), :] + af_ref[g, seg_rows(t0), :] * cf[g, s:s + 1, :]
                     + bb_ref[g, seg_rows(t0), :] + ab_ref[g, seg_rows(t0), :] * cb[g, s:s + 1, :])
                y = h * _gelu_tanh(ug_ref[0, t0:t0 + piece, lanes].astype(F32))
                yr_ref[0, t0:t0 + piece, lanes] = y.astype(yr_ref.dtype)


def _lru_scan_call(xc, ug, h0, lw):
    b, l, w = xc.shape
    seq = pl.BlockSpec((1, l, w), lambda bi: (bi, 0, 0))
    st = pl.BlockSpec((1, 2, w), lambda bi: (bi, 0, 0))
    consts = [lw["lru_gate_w"], lw["lru_gate_b"], lw["lru_lam"]]
    with_output = ug is not None
    st_shape = jax.ShapeDtypeStruct((b, 2, w), F32)
    out = pl.pallas_call(
        functools.partial(_lru_scan_body, chunk=min(256, l), with_output=with_output),
        grid=(b,),
        in_specs=[seq] * (2 if with_output else 1) + [st] + [_const_spec(a.shape) for a in consts],
        out_specs=[seq, st] if with_output else st,
        out_shape=[jax.ShapeDtypeStruct((b, l, w), BF16), st_shape] if with_output else st_shape,
        scratch_shapes=[pltpu.VMEM((w // LANES, l, LANES), F32)] * 4,
        compiler_params=_params(1),
        name="lru_scan",
    )(*([xc, ug] if with_output else [xc]), h0, *consts)
    return out if with_output else (None, out)


def _fourier_seq_body(z_ref, dc_ref, ds_ref, twc_ref, tws_ref, o_ref, fold_ref, il_ref, *, chunk):
    l, fw = o_ref.shape[1], o_ref.shape[2]
    half = l // 2
    for i in range(half // chunk):
        rows = slice(i * chunk, (i + 1) * chunk)
        lo = z_ref[0, rows, :].astype(F32)
        hi = z_ref[0, half + i * chunk:half + (i + 1) * chunk, :].astype(F32)
        tot, dif = lo + hi, lo - hi
        dre, dim = dif[:, 0:fw], dif[:, fw:2 * fw]
        cos, sin = twc_ref[rows, :], tws_ref[rows, :]
        fold_ref[0, rows, :] = tot[:, 0:fw].astype(fold_ref.dtype)
        fold_ref[1, rows, :] = tot[:, fw:2 * fw].astype(fold_ref.dtype)
        fold_ref[2, rows, :] = (dre * cos - dim * sin).astype(fold_ref.dtype)
        fold_ref[3, rows, :] = (dre * sin + dim * cos).astype(fold_ref.dtype)
    for i in range(half // chunk):
        rows = slice(i * chunk, (i + 1) * chunk)
        even = _dot(dc_ref[rows, :], fold_ref[0]) + _dot(ds_ref[rows, :], fold_ref[1])
        odd = _dot(dc_ref[rows, :], fold_ref[2]) + _dot(ds_ref[rows, :], fold_ref[3])
        for g in range(fw // LANES):
            lanes = slice(g * LANES, (g + 1) * LANES)
            il_ref[g, pl.ds(2 * i * chunk, chunk, stride=2), :] = even[:, lanes]
            il_ref[g, pl.ds(2 * i * chunk + 1, chunk, stride=2), :] = odd[:, lanes]
            out_rows = slice(2 * i * chunk, 2 * (i + 1) * chunk)
            o_ref[0, out_rows, lanes] = il_ref[g, out_rows, :].astype(o_ref.dtype)


def _fourier_seq_call(zcs, dc, ds, twc, tws):
    b, l, w2 = zcs.shape
    fw = w2 // 2
    consts = [dc, ds, twc, tws]
    return pl.pallas_call(
        functools.partial(_fourier_seq_body, chunk=min(256, l // 2)),
        grid=(b,),
        in_specs=[pl.BlockSpec((1, l, w2), lambda bi: (bi, 0, 0))] + [_const_spec(a.shape) for a in consts],
        out_specs=pl.BlockSpec((1, l, fw), lambda bi: (bi, 0, 0)),
        out_shape=jax.ShapeDtypeStruct((b, l, fw), BF16),
        scratch_shapes=[pltpu.VMEM((4, l // 2, fw), BF16), pltpu.VMEM((fw // LANES, l, LANES), F32)],
        compiler_params=_params(1),
        name="fourier_seq",
    )(zcs, *consts)


def _merge_ffn_body(h_ref, f_ref, yr_ref, g01_ref, ycg_ref, mod_ref, g2_ref, fow_ref, low_ref, mow_ref,
                    upw_ref, wprev_ref, wmid_ref, wnext_ref, fcb_ref, dww_ref, fin_ref, o_ref, act_ref,
                    *, row, final_norm):
    tl, d = h_ref.shape[1], h_ref.shape[2]
    hidden = dww_ref.shape[0]
    hc = HIDDEN_CHUNK
    r = tl // row
    mod = mod_ref[0]
    gate1, sh2, sc2, gate2 = (mod[:, 2 * d:3 * d], mod[:, 3 * d:4 * d], mod[:, 4 * d:5 * d], mod[:, 5 * d:6 * d])

    halves = [slice(0, tl)] if tl < 2 * MXU_TILE else [slice(0, tl // 2), slice(tl // 2, tl)]
    h1_parts, nb_parts = [], []
    for rows in halves:
        yf = _dot(f_ref[0, rows, :], fow_ref[...])
        yr = _dot(yr_ref[0, rows, :], low_ref[...])
        merged = g01_ref[0, rows, 0:d] * yf + g01_ref[0, rows, d:2 * d] * yr + ycg_ref[0, rows, :]
        h1_parts.append(h_ref[0, rows, :] + gate1 * _dot(merged.astype(BF16), mow_ref[...]))
        nb_parts.append((_rms(h1_parts[-1], g2_ref[...]) * (1.0 + sc2) + sh2).astype(BF16))
    h1 = jnp.concatenate(h1_parts, axis=0)
    nb = jnp.concatenate(nb_parts, axis=0)

    def conv3(u, lo):
        cols = slice(lo, lo + hc)
        prev = pltpu.roll(u, 1, 0).reshape(r, row, hc) * wprev_ref[:, cols][None]
        nxt = pltpu.roll(u, tl - 1, 0).reshape(r, row, hc) * wnext_ref[:, cols][None]
        out = u.reshape(r, row, hc) * wmid_ref[:, cols][None] + prev + nxt + fcb_ref[:, cols][None]
        return out.reshape(tl, hc)

    for j in range(hidden // hc):
        lo = j * hc
        v = conv3(_dot(nb, upw_ref[:, lo:lo + hc]), lo)
        g = conv3(_dot(nb, upw_ref[:, hidden + lo:hidden + lo + hc]), hidden + lo)
        act_ref[:, lo:lo + hc] = (g * jax.nn.sigmoid(g) * v).astype(act_ref.dtype)
    h2 = h1 + gate2 * _dot(act_ref[...], dww_ref[...])
    if final_norm:
        h2 = _rms(h2, fin_ref[...])
    o_ref[0] = h2


def _merge_ffn_call(h, f, yr, g01, ycg, mod, lw, row, shared_mod, final_norm):
    b, l, d = h.shape
    tl, tok, mod_spec = _token_specs(l, shared_mod, mod)
    conv_w = lw["ffn_conv_w"]
    pos = jnp.arange(row)[:, None]
    wprev = jnp.where(pos != 0, conv_w[0:1], 0.0)
    wnext = jnp.where(pos != row - 1, conv_w[2:3], 0.0)
    consts = [lw["norm2_g"], lw["fourier_out_w"], lw["lru_out_w"], lw["mix_out_w"], lw["ffn_up_w"],
              wprev, conv_w[1:2], wnext, lw["ffn_conv_b"], lw["ffn_down_w"], lw["final_g"]]
    hidden = lw["ffn_down_w"].shape[0]
    return pl.pallas_call(
        functools.partial(_merge_ffn_body, row=row, final_norm=final_norm),
        grid=(b, l // tl),
        in_specs=[tok(d), tok(f.shape[2]), tok(yr.shape[2]), tok(2 * d), tok(d), mod_spec]
        + [_const_spec(a.shape) for a in consts],
        out_specs=tok(d),
        out_shape=jax.ShapeDtypeStruct((b, l, d), F32),
        scratch_shapes=[pltpu.VMEM((tl, hidden), BF16)],
        compiler_params=_params(2),
        name="merge_ffn",
    )(h, f, yr, g01, ycg, mod, *consts)


def _dft_constants(l, group_width, groups):
    half = l // 2
    k = np.arange(half, dtype=np.int64)
    ang = 2.0 * np.pi * ((k[:, None] * k[None, :]) % half).astype(np.float64) / half
    scale = 1.0 / np.sqrt(float(l) * group_width)
    dc = np.cos(ang) * scale
    ds = -np.sin(ang) * scale
    phi = (2.0 * np.pi * k.astype(np.float64) / l)[:, None] * np.ones((1, group_width * groups))
    twiddles = (jnp.asarray(np.cos(phi), F32), jnp.asarray(np.sin(phi), F32))
    c = np.arange(group_width, dtype=np.int64)
    cang = 2.0 * np.pi * ((c[:, None] * c[None, :]) % group_width).astype(np.float64) / group_width
    eye = np.eye(max(1, min(groups, MXU_TILE // group_width)))
    chan = np.concatenate([np.kron(eye, np.cos(cang)), np.kron(eye, np.sin(cang))], axis=1)
    return (jnp.asarray(dc, BF16), jnp.asarray(ds, BF16)) + twiddles, jnp.asarray(chan, BF16)


def _block_diag(w):
    h, i, j = w.shape
    eye = jnp.eye(h, dtype=w.dtype)
    return (eye[:, None, :, None] * w[:, :, None, :]).reshape(h * i, h * j)


def kernel(x, c, ctx, c_ctx, ada_w, ada_b, norm1_g, norm2_g, in_w, in_b, fourier_out_w, lru_conv_w, lru_conv_b, lru_wa, lru_ba, lru_wx, lru_bx, lru_lam, lru_out_w, conf_conv_w, conf_conv_b, conf_ln_g, conf_ln_b, conf_out_w, mix_out_w, ffn_up_w, ffn_conv_w, ffn_conv_b, ffn_down_w, final_g):
    batch, seq, d = x.shape
    ctx_len = ctx.shape[1]
    depth = ada_w.shape[0]
    fw = fourier_out_w.shape[1]
    lwid = lru_out_w.shape[1]
    cw = conf_out_w.shape[1]
    assert batch + 1 <= MOD_ROWS and seq % GRID_W == 0
    assert lwid == cw

    cs = jnp.zeros((MOD_ROWS, d), F32).at[:batch].set(c).at[batch].set(c_ctx)
    mod = _ada_call(cs, ada_w, ada_b)

    dft = {}
    for length in (seq, ctx_len):
        dft[length], chan_dft = _dft_constants(length, fw // FOURIER_GROUPS, FOURIER_GROUPS)

    off_c = fw + 2 * lwid

    def slice_glu_columns(a):
        glu = a[..., off_c:off_c + 2 * cw]
        glu = glu.reshape(a.shape[:-1] + (2, cw // LANES, LANES)).swapaxes(-3, -2).reshape(glu.shape)
        return jnp.concatenate([a[..., :off_c], glu, a[..., off_c + 2 * cw:]], axis=-1)

    row2 = lambda a: a.reshape(1, -1)
    h_lat, h_ctx = x, ctx
    zero_state = jnp.zeros((batch, 2, lwid), F32)
    for i in range(depth):
        heads_per_tile = MXU_TILE // lru_wa.shape[-1]
        gate_w = jnp.stack([
            jnp.stack([_block_diag(w[t:t + heads_per_tile]) for t in range(0, w.shape[0], heads_per_tile)])
            for w in (lru_wa[i, 0], lru_wx[i, 0], lru_wa[i, 1], lru_wx[i, 1])])
        gate_b = jnp.concatenate([lru_ba[i, 0], lru_bx[i, 0], lru_ba[i, 1], lru_bx[i, 1]])
        lw = dict(fw=fw, lw=lwid, cw=cw,
                  norm1_g=row2(norm1_g[i]), norm2_g=row2(norm2_g[i]),
                  in_w=slice_glu_columns(in_w[i].astype(BF16)), in_b=row2(slice_glu_columns(in_b[i])),
                  chan_dft=chan_dft, lru_conv_w=lru_conv_w[i], lru_conv_b=row2(lru_conv_b[i]),
                  conf_conv_w=conf_conv_w[i], conf_conv_b=row2(conf_conv_b[i]),
                  conf_ln_g=row2(conf_ln_g[i]), conf_ln_b=row2(conf_ln_b[i]), conf_out_w=conf_out_w[i].astype(BF16),
                  lru_gate_w=(0.5 * gate_w).astype(BF16), lru_gate_b=row2(0.5 * gate_b), lru_lam=lru_lam[i],
                  fourier_out_w=fourier_out_w[i].astype(BF16), lru_out_w=lru_out_w[i].astype(BF16),
                  mix_out_w=mix_out_w[i].astype(BF16), ffn_up_w=ffn_up_w[i].astype(BF16),
                  ffn_conv_w=ffn_conv_w[i], ffn_conv_b=row2(ffn_conv_b[i]), ffn_down_w=ffn_down_w[i].astype(BF16),
                  final_g=row2(final_g))
        last = i == depth - 1
        mod_lat = mod[i, :batch].reshape(batch, 1, -1)
        mod_ctx = mod[i, batch:batch + 1].reshape(1, 1, -1)

        flat = lambda a: a.reshape(1, batch * ctx_len, a.shape[-1])
        per_seq = lambda a: a.reshape(batch, ctx_len, a.shape[-1])
        if last:
            xc_c = _lru_input_call(flat(h_ctx), mod_ctx, lw, ctx_len, True)
            _, state_c = _lru_scan_call(per_seq(xc_c), None, zero_state, lw)
        else:
            zcs_c, xc_c, ug_c, g01_c, ycg_c = _mixer_local_call(flat(h_ctx), mod_ctx, lw, ctx_len, True)
            yr_c, state_c = _lru_scan_call(per_seq(xc_c), per_seq(ug_c), zero_state, lw)
            f_c = _fourier_seq_call(per_seq(zcs_c), *dft[ctx_len])
            h_ctx = per_seq(_merge_ffn_call(flat(h_ctx), flat(f_c), flat(yr_c), g01_c, ycg_c, mod_ctx, lw,
                                            ctx_len, True, False))

        zcs, xc, ug, g01, ycg = _mixer_local_call(h_lat, mod_lat, lw, GRID_W, False)
        f = _fourier_seq_call(zcs, *dft[seq])
        yr, _ = _lru_scan_call(xc, ug, state_c, lw)
        h_lat = _merge_ffn_call(h_lat, f, yr, g01, ycg, mod_lat, lw, GRID_W, False, last)
    return h_lat
```

```python
import functools

import numpy as np
import jax
import jax.numpy as jnp
from jax import lax
from jax.experimental import pallas as pl
from jax.experimental.pallas import tpu as pltpu

F32 = jnp.float32
BF16 = jnp.bfloat16

GRID_W = 64
FOURIER_GROUPS = 4
LRU_HEADS = 8
LRU_C = 8.0
N_BRANCH = 3
EPS = 1e-6
LN_EPS = 1e-5

VMEM_LIMIT_BYTES = 56 * 1024 * 1024
SUBLANES = 8
CONV_CHUNK = 64
PAD_ROWS = 16
TOKEN_TILE = 512
HIDDEN_CHUNK = 256
GATE_CHUNK = 256
LANES = 128
MXU_TILE = 256
MOD_ROWS = 24


def _const_spec(shape):
    nd = len(shape)
    return pl.BlockSpec(shape, lambda *_: (0,) * nd, pipeline_mode=pl.Buffered(1))


def _params(n_grid):
    return pltpu.CompilerParams(dimension_semantics=("arbitrary",) * n_grid,
                                vmem_limit_bytes=VMEM_LIMIT_BYTES)


def _dot(a, b):
    return jnp.dot(a, b, preferred_element_type=F32)


def _gelu_tanh(x):
    c1 = float(np.sqrt(2.0 / np.pi))
    half = 0.5 * x
    return half + half * jnp.tanh(x * (c1 + (c1 * 0.044715) * (x * x)))


def _silu(x):
    half = 0.5 * x
    return half + half * jnp.tanh(half)


def _rms(x, g):
    return x * lax.rsqrt(jnp.mean(x * x, axis=-1, keepdims=True) + EPS) * g


def _ada_body(c_ref, w_ref, b_ref, o_ref):
    c = c_ref[...]
    s = c * jax.nn.sigmoid(c)
    o_ref[0] = jnp.dot(s, w_ref[0], preferred_element_type=F32,
                       precision=lax.Precision.HIGHEST) + b_ref[0]


def _ada_call(cs, ada_w, ada_b):
    depth, d, n = ada_w.shape
    nblk = 1536
    return pl.pallas_call(
        _ada_body,
        grid=(depth, n // nblk),
        in_specs=[pl.BlockSpec((MOD_ROWS, d), lambda l, j: (0, 0)),
                  pl.BlockSpec((1, d, nblk), lambda l, j: (l, 0, j)),
                  pl.BlockSpec((1, 1, nblk), lambda l, j: (l, 0, j))],
        out_specs=pl.BlockSpec((1, MOD_ROWS, nblk), lambda l, j: (l, 0, j)),
        out_shape=jax.ShapeDtypeStruct((depth, MOD_ROWS, n), F32),
        compiler_params=_params(2),
        name="adaln_mod",
    )(cs, ada_w, ada_b.reshape(depth, 1, n))


def _fill_padded(pad_ref, val, row):
    tl, c = val.shape
    pad_ref[:, PAD_ROWS:PAD_ROWS + row, :] = val.reshape(tl // row, row, c)


def _row_conv_short(pad_ref, val, w_ref, left, row):
    _fill_padded(pad_ref, val, row)
    pieces = []
    for ri in range(val.shape[0] // row):
        for p in range(row // CONV_CHUNK):
            base = PAD_ROWS + p * CONV_CHUNK - left
            acc = pad_ref[ri, base:base + CONV_CHUNK, :] * w_ref[0:1, :]
            for j in range(1, w_ref.shape[0]):
                acc = acc + pad_ref[ri, base + j:base + j + CONV_CHUNK, :] * w_ref[j:j + 1, :]
            pieces.append(acc)
    return jnp.concatenate(pieces, axis=0)


def _tap_offset(left, j, p):
    return divmod(PAD_ROWS - left + j + p * CONV_CHUNK, SUBLANES)


def _build_phase_copies(pad_ref, shift_ref, ri, n_taps, left, lanes=slice(None)):
    span = shift_ref.shape[1]
    for s in sorted({_tap_offset(left, j, 0)[1] for j in range(n_taps)} - {0}):
        shift_ref[s - 1, :, lanes] = pad_ref[ri, s:s + span, lanes]


def _row_conv_wide_chunk(pad_ref, shift_ref, w_ref, left, ri, p, lanes=slice(None)):
    acc = None
    for j in range(w_ref.shape[0]):
        q, s = _tap_offset(left, j, p)
        rows = slice(q * SUBLANES, q * SUBLANES + CONV_CHUNK)
        src = pad_ref[ri, rows, lanes] if s == 0 else shift_ref[s - 1, rows, lanes]
        term = src * w_ref[j:j + 1, lanes]
        acc = term if acc is None else acc + term
    return acc


def _zero_margins(pad_ref, row):
    zeros = jnp.zeros((pad_ref.shape[0], PAD_ROWS, pad_ref.shape[2]), F32)
    pad_ref[:, 0:PAD_ROWS, :] = zeros
    pad_ref[:, PAD_ROWS + row:, :] = zeros


def _normed_input(h_ref, mod_ref, g_ref, d):
    mod = mod_ref[0]
    n = _rms(h_ref[0], g_ref[...]) * (1.0 + mod[:, d:2 * d]) + mod[:, 0:d]
    return n.astype(BF16)


def _mixer_local_body(h_ref, mod_ref, g_ref, w_ref, b_ref, cs_ref, lcw_ref, lcb_ref,
                      ccw_ref, ccb_ref, lng_ref, lnb_ref, cow_ref,
                      zcs_ref, xc_ref, ug_ref, g01_ref, ycg_ref, pad_ref, shift_ref, conv_ref, gate_ref,
                      *, row, dims):
    d, fw, lw, cw = dims
    off_lx = fw
    off_lg = off_lx + lw
    off_c = off_lg + lw
    off_g = off_c + 2 * cw
    tl = h_ref.shape[1]

    _zero_margins(pad_ref, row)
    nb = _normed_input(h_ref, mod_ref, g_ref, d)

    def proj(lo, hi):
        return _dot(nb, w_ref[:, lo:hi]) + b_ref[:, lo:hi]

    ux = proj(off_lx, off_lg)
    units = [(ri, p) for ri in range(tl // row) for p in range(row // CONV_CHUNK)]
    n_taps = lcw_ref.shape[0]
    for c in range(lw // LANES):
        lanes = slice(c * LANES, (c + 1) * LANES)
        pad_ref[:, PAD_ROWS:PAD_ROWS + row, lanes] = ux[:, lanes].reshape(tl // row, row, LANES)
        for u, (ri, p) in enumerate(units):
            if p == 0:
                _build_phase_copies(pad_ref, shift_ref, ri, n_taps, n_taps // 2, lanes)
            xc = _row_conv_wide_chunk(pad_ref, shift_ref, lcw_ref, n_taps // 2, ri, p, lanes) + lcb_ref[:, lanes]
            xc_ref[0, u * CONV_CHUNK:(u + 1) * CONV_CHUNK, lanes] = xc.astype(xc_ref.dtype)

    def fourier_job():
        uf = proj(0, fw).astype(BF16)
        kt = cs_ref.shape[0]
        for t in range(fw // kt):
            z = _dot(uf[:, t * kt:(t + 1) * kt], cs_ref[...]).astype(zcs_ref.dtype)
            zcs_ref[0, :, t * kt:(t + 1) * kt] = z[:, 0:kt]
            zcs_ref[0, :, fw + t * kt:fw + (t + 1) * kt] = z[:, kt:2 * kt]

    def gate_branch_job():
        ug_ref[0] = proj(off_lg, off_c).astype(ug_ref.dtype)

    def gate_job(lo):
        s = jax.nn.sigmoid(proj(off_g + lo, off_g + lo + GATE_CHUNK))
        if lo < 2 * d:
            g01_ref[0, :, lo:lo + GATE_CHUNK] = s.astype(g01_ref.dtype)
        else:
            gate_ref[:, lo - 2 * d:lo - 2 * d + GATE_CHUNK] = s

    def conformer_job(c):
        lanes = slice(c * LANES, (c + 1) * LANES)
        u = proj(off_c + 2 * c * LANES, off_c + 2 * (c + 1) * LANES)
        glu = u[:, :LANES] * jax.nn.sigmoid(u[:, LANES:])
        pad_ref[:, PAD_ROWS:PAD_ROWS + row, lanes] = glu.reshape(tl // row, row, LANES)
        n_taps = ccw_ref.shape[0]
        for ui, (ri, p) in enumerate(units):
            if p == 0:
                _build_phase_copies(pad_ref, shift_ref, ri, n_taps, n_taps // 2, lanes)
            v = _row_conv_wide_chunk(pad_ref, shift_ref, ccw_ref, n_taps // 2, ri, p, lanes) + ccb_ref[:, lanes]
            conv_ref[ui * CONV_CHUNK:(ui + 1) * CONV_CHUNK, lanes] = v

    fourier_job()
    gate_branch_job()
    gate_los = list(range(0, N_BRANCH * d, GATE_CHUNK))
    n_slices = cw // LANES
    per_slice = -(-len(gate_los) // n_slices)
    for c in range(n_slices):
        conformer_job(c)
        for lo in gate_los[c * per_slice:(c + 1) * per_slice]:
            gate_job(lo)

    v = conv_ref[...]
    mu = jnp.mean(v, axis=-1, keepdims=True)
    vc = v - mu
    var = jnp.mean(vc * vc, axis=-1, keepdims=True)
    v = vc * lax.rsqrt(var + LN_EPS) * lng_ref[...] + lnb_ref[...]
    yc = _dot(_silu(v).astype(BF16), cow_ref[...])
    ycg_ref[0] = (gate_ref[...] * yc).astype(ycg_ref.dtype)


def _lru_input_body(h_ref, mod_ref, g_ref, w_ref, b_ref, lcw_ref, lcb_ref, xc_ref, pad_ref, *, row, d):
    _zero_margins(pad_ref, row)
    nb = _normed_input(h_ref, mod_ref, g_ref, d)
    ux = _dot(nb, w_ref[...]) + b_ref[...]
    xc = _row_conv_short(pad_ref, ux, lcw_ref, lcw_ref.shape[0] // 2, row) + lcb_ref[...]
    xc_ref[0] = xc.astype(xc_ref.dtype)


def _token_specs(l, shared_mod, mod):
    tl = min(TOKEN_TILE, l)
    mod_map = (lambda bi, ti: (0, 0, 0)) if shared_mod else (lambda bi, ti: (bi, 0, 0))
    tok = lambda width: pl.BlockSpec((1, tl, width), lambda bi, ti: (bi, ti, 0))
    return tl, tok, pl.BlockSpec((1, 1, mod.shape[2]), mod_map)


def _mixer_local_call(h, mod, lw, row, shared_mod):
    b, l, d = h.shape
    fw, lwid, cw = lw["fw"], lw["lw"], lw["cw"]
    tl, tok, mod_spec = _token_specs(l, shared_mod, mod)
    body = functools.partial(_mixer_local_body, row=row, dims=(d, fw, lwid, cw))
    consts = [lw["norm1_g"], lw["in_w"], lw["in_b"], lw["chan_dft"], lw["lru_conv_w"], lw["lru_conv_b"],
              lw["conf_conv_w"], lw["conf_conv_b"], lw["conf_ln_g"], lw["conf_ln_b"], lw["conf_out_w"]]
    widths = [2 * fw, lwid, lwid, 2 * d, d]
    return pl.pallas_call(
        body,
        grid=(b, l // tl),
        in_specs=[tok(d), mod_spec] + [_const_spec(a.shape) for a in consts],
        out_specs=[tok(w) for w in widths],
        out_shape=[jax.ShapeDtypeStruct((b, l, w), BF16) for w in widths],
        scratch_shapes=[pltpu.VMEM((tl // row, row + 2 * PAD_ROWS, max(lwid, cw)), F32),
                        pltpu.VMEM((SUBLANES - 1, row + 2 * PAD_ROWS - SUBLANES, cw), F32),
                        pltpu.VMEM((tl, cw), F32),
                        pltpu.VMEM((tl, d), F32)],
        compiler_params=_params(2),
        name="mixer_local",
    )(h, mod, *consts)


def _lru_input_call(h, mod, lw, row, shared_mod):
    b, l, d = h.shape
    fw, lwid = lw["fw"], lw["lw"]
    tl, tok, mod_spec = _token_specs(l, shared_mod, mod)
    consts = [lw["norm1_g"], lw["in_w"][:, fw:fw + lwid], lw["in_b"][:, fw:fw + lwid],
              lw["lru_conv_w"], lw["lru_conv_b"]]
    return pl.pallas_call(
        functools.partial(_lru_input_body, row=row, d=d),
        grid=(b, l // tl),
        in_specs=[tok(d), mod_spec] + [_const_spec(a.shape) for a in consts],
        out_specs=tok(lwid),
        out_shape=jax.ShapeDtypeStruct((b, l, lwid), BF16),
        scratch_shapes=[pltpu.VMEM((tl // row, row + 2 * PAD_ROWS, lwid), F32)],
        compiler_params=_params(2),
        name="lru_input",
    )(h, mod, *consts)


def _segment_carries(h_fin, p_fin, h0, reverse):
    rows = [None] * SUBLANES
    c = h0
    for s in (range(SUBLANES - 1, -1, -1) if reverse else range(SUBLANES)):
        rows[s] = c
        c = h_fin[:, s:s + 1, :] + p_fin[:, s:s + 1, :] * c
    return jnp.concatenate(rows, axis=1), c


def _lru_scan_body(*refs, chunk, with_output):
    if with_output:
        xc_ref, ug_ref, h0_ref, wg_ref, bg_ref, lam_ref, yr_ref, hfin_ref, af_ref, bf_ref, ab_ref, bb_ref = refs
    else:
        xc_ref, h0_ref, wg_ref, bg_ref, lam_ref, hfin_ref, af_ref, bf_ref, ab_ref, bb_ref = refs
    l, w = xc_ref.shape[1], xc_ref.shape[2]
    groups = w // LANES
    seg = l // SUBLANES
    piece = min(chunk, seg)
    neg = -lam_ref[...]
    softplus = jnp.maximum(neg, 0.0) + jnp.log1p(jnp.exp(-jnp.abs(neg)))
    half_rate = (-0.5 * LRU_C * np.log2(np.e)) * softplus

    def seg_rows(t0):
        s, k0 = divmod(t0, seg)
        return pl.ds(SUBLANES * k0 + s, piece, stride=SUBLANES)

    def scatter(ref, val, t0):
        for q in range(val.shape[0] // piece):
            for g in range(groups):
                ref[g, seg_rows(t0 + q * piece), :] = val[q * piece:(q + 1) * piece, g * LANES:(g + 1) * LANES]

    for i in range(l // chunk):
        rows = slice(i * chunk, (i + 1) * chunk)
        xb = xc_ref[0, rows, :]
        half_x = 0.5 * xb.astype(F32)

        def gate_tanh(gi):
            kt = wg_ref.shape[2]
            pre = [_dot(xb[:, t * kt:(t + 1) * kt], wg_ref[gi, t]) for t in range(w // kt)]
            return jnp.tanh(jnp.concatenate(pre, axis=1) + bg_ref[:, gi * w:(gi + 1) * w])

        for di, (a_ref, b_ref) in enumerate(((af_ref, bf_ref), (ab_ref, bb_ref))):
            rate = half_rate[di:di + 1, :]
            a = jnp.exp2(gate_tanh(2 * di) * rate + rate)
            gated_x = (gate_tanh(2 * di + 1) + 1.0) * half_x
            scatter(a_ref, a, i * chunk)
            y = 1.0 - a * a
            root = jnp.where(y > 0.0, y * lax.rsqrt(y), 0.0)
            scatter(b_ref, root * gated_x, i * chunk)

    def step(k, carry):
        hf, pf, hb, pb = carry
        rf = pl.ds(pl.multiple_of(k * SUBLANES, SUBLANES), SUBLANES)
        rb = pl.ds(pl.multiple_of((seg - 1 - k) * SUBLANES, SUBLANES), SUBLANES)
        a = af_ref[:, rf, :]
        hf = a * hf + bf_ref[:, rf, :]
        pf = a * pf
        bf_ref[:, rf, :] = hf
        af_ref[:, rf, :] = pf
        a = ab_ref[:, rb, :]
        hb = a * hb + bb_ref[:, rb, :]
        pb = a * pb
        bb_ref[:, rb, :] = hb
        ab_ref[:, rb, :] = pb
        return hf, pf, hb, pb

    zeros = jnp.zeros((groups, SUBLANES, LANES), F32)
    ones = jnp.ones((groups, SUBLANES, LANES), F32)
    hf, pf, hb, pb = lax.fori_loop(0, seg, step, (zeros, ones, zeros, ones), unroll=8)

    h0 = h0_ref[0]
    split = lambda row: jnp.stack([row[:, g * LANES:(g + 1) * LANES] for g in range(groups)])
    cf, end_f = _segment_carries(hf, pf, split(h0[0:1, :]), False)
    cb, end_b = _segment_carries(hb, pb, split(h0[1:2, :]), True)
    for g in range(groups):
        hfin_ref[0, 0:1, g * LANES:(g + 1) * LANES] = end_f[g]
        hfin_ref[0, 1:2, g * LANES:(g + 1) * LANES] = end_b[g]

    if with_output:
        blk = min(chunk, l)
        tiles = blk // SUBLANES
        as_tiles = lambda ref, rows: ref[:, rows, :].reshape(groups, tiles, SUBLANES, LANES)
        for r0 in range(0, l, blk):
            rows = slice(r0, r0 + blk)
            total = (as_tiles(bf_ref, rows) + as_tiles(af_ref, rows) * cf[:, None]
                     + as_tiles(bb_ref, rows) + as_tiles(ab_ref, rows) * cb[:, None])
            bf_ref[:, rows, :] = total.reshape(groups, blk, LANES)
        for t0 in range(0, l, piece):
            for g in range(groups):
                lanes = slice(g * LANES, (g + 1) * LANES)
                y = bf_ref[g, seg_rows(t0), :] * _gelu_tanh(ug_ref[0, t0:t0 + piece, lanes].astype(F32))
                yr_ref[0, t0:t0 + piece, lanes] = y.astype(yr_ref.dtype)


def _lru_scan_call(xc, ug, h0, lw):
    b, l, w = xc.shape
    seq = pl.BlockSpec((1, l, w), lambda bi: (bi, 0, 0))
    st = pl.BlockSpec((1, 2, w), lambda bi: (bi, 0, 0))
    consts = [lw["lru_gate_w"], lw["lru_gate_b"], lw["lru_lam"]]
    with_output = ug is not None
    st_shape = jax.ShapeDtypeStruct((b, 2, w), F32)
    out = pl.pallas_call(
        functools.partial(_lru_scan_body, chunk=min(256, l), with_output=with_output),
        grid=(b,),
        in_specs=[seq] * (2 if with_output else 1) + [st] + [_const_spec(a.shape) for a in consts],
        out_specs=[seq, st] if with_output else st,
        out_shape=[jax.ShapeDtypeStruct((b, l, w), BF16), st_shape] if with_output else st_shape,
        scratch_shapes=[pltpu.VMEM((w // LANES, l, LANES), F32)] * 4,
        compiler_params=_params(1),
        name="lru_scan",
    )(*([xc, ug] if with_output else [xc]), h0, *consts)
    return out if with_output else (None, out)


def _fourier_seq_body(z_ref, dc_ref, ds_ref, twc_ref, tws_ref, o_ref, fold_ref, il_ref, *, chunk):
    l, fw = o_ref.shape[1], o_ref.shape[2]
    half = l // 2
    for i in range(half // chunk):
        rows = slice(i * chunk, (i + 1) * chunk)
        lo = z_ref[0, rows, :].astype(F32)
        hi = z_ref[0, half + i * chunk:half + (i + 1) * chunk, :].astype(F32)
        tot, dif = lo + hi, lo - hi
        dre, dim = dif[:, 0:fw], dif[:, fw:2 * fw]
        cos, sin = twc_ref[rows, :], tws_ref[rows, :]
        fold_ref[0, rows, :] = tot[:, 0:fw].astype(fold_ref.dtype)
        fold_ref[1, rows, :] = tot[:, fw:2 * fw].astype(fold_ref.dtype)
        fold_ref[2, rows, :] = (dre * cos - dim * sin).astype(fold_ref.dtype)
        fold_ref[3, rows, :] = (dre * sin + dim * cos).astype(fold_ref.dtype)
    for i in range(half // chunk):
        rows = slice(i * chunk, (i + 1) * chunk)
        even = _dot(dc_ref[rows, :], fold_ref[0]) + _dot(ds_ref[rows, :], fold_ref[1])
        odd = _dot(dc_ref[rows, :], fold_ref[2]) + _dot(ds_ref[rows, :], fold_ref[3])
        for g in range(fw // LANES):
            lanes = slice(g * LANES, (g + 1) * LANES)
            il_ref[g, pl.ds(2 * i * chunk, chunk, stride=2), :] = even[:, lanes]
            il_ref[g, pl.ds(2 * i * chunk + 1, chunk, stride=2), :] = odd[:, lanes]
            out_rows = slice(2 * i * chunk, 2 * (i + 1) * chunk)
            o_ref[0, out_rows, lanes] = il_ref[g, out_rows, :].astype(o_ref.dtype)


def _fourier_seq_call(zcs, dc, ds, twc, tws):
    b, l, w2 = zcs.shape
    fw = w2 // 2
    consts = [dc, ds, twc, tws]
    return pl.pallas_call(
        functools.partial(_fourier_seq_body, chunk=min(256, l // 2)),
        grid=(b,),
        in_specs=[pl.BlockSpec((1, l, w2), lambda bi: (bi, 0, 0))] + [_const_spec(a.shape) for a in consts],
        out_specs=pl.BlockSpec((1, l, fw), lambda bi: (bi, 0, 0)),
        out_shape=jax.ShapeDtypeStruct((b, l, fw), BF16),
        scratch_shapes=[pltpu.VMEM((4, l // 2, fw), BF16), pltpu.VMEM((fw // LANES, l, LANES), F32)],
        compiler_params=_params(1),
        name="fourier_seq",
    )(zcs, *consts)


def _merge_ffn_body(h_ref, f_ref, yr_ref, g01_ref, ycg_ref, mod_ref, g2_ref, fow_ref, low_ref, mow_ref,
                    upw_ref, wprev_ref, wmid_ref, wnext_ref, fcb_ref, dww_ref, fin_ref, o_ref, act_ref,
                    *, row, final_norm):
    tl, d = h_ref.shape[1], h_ref.shape[2]
    hidden = dww_ref.shape[0]
    hc = HIDDEN_CHUNK
    r = tl // row
    mod = mod_ref[0]
    gate1, sh2, sc2, gate2 = (mod[:, 2 * d:3 * d], mod[:, 3 * d:4 * d], mod[:, 4 * d:5 * d], mod[:, 5 * d:6 * d])

    halves = [slice(0, tl)] if tl < 2 * MXU_TILE else [slice(0, tl // 2), slice(tl // 2, tl)]
    h1_parts, nb_parts = [], []
    for rows in halves:
        yf = _dot(f_ref[0, rows, :], fow_ref[...])
        yr = _dot(yr_ref[0, rows, :], low_ref[...])
        merged = g01_ref[0, rows, 0:d] * yf + g01_ref[0, rows, d:2 * d] * yr + ycg_ref[0, rows, :]
        h1_parts.append(h_ref[0, rows, :] + gate1 * _dot(merged.astype(BF16), mow_ref[...]))
        nb_parts.append((_rms(h1_parts[-1], g2_ref[...]) * (1.0 + sc2) + sh2).astype(BF16))
    h1 = jnp.concatenate(h1_parts, axis=0)
    nb = jnp.concatenate(nb_parts, axis=0)

    def conv3(u, lo):
        cols = slice(lo, lo + hc)
        prev = pltpu.roll(u, 1, 0).reshape(r, row, hc) * wprev_ref[:, cols][None]
        nxt = pltpu.roll(u, tl - 1, 0).reshape(r, row, hc) * wnext_ref[:, cols][None]
        out = u.reshape(r, row, hc) * wmid_ref[:, cols][None] + prev + nxt + fcb_ref[:, cols][None]
        return out.reshape(tl, hc)

    for j in range(hidden // hc):
        lo = j * hc
        v = conv3(_dot(nb, upw_ref[:, lo:lo + hc]), lo)
        g = conv3(_dot(nb, upw_ref[:, hidden + lo:hidden + lo + hc]), hidden + lo)
        act_ref[:, lo:lo + hc] = (_silu(g) * v).astype(act_ref.dtype)
    h2 = h1 + gate2 * _dot(act_ref[...], dww_ref[...])
    if final_norm:
        h2 = _rms(h2, fin_ref[...])
    o_ref[0] = h2


def _merge_ffn_call(h, f, yr, g01, ycg, mod, lw, row, shared_mod, final_norm):
    b, l, d = h.shape
    tl, tok, mod_spec = _token_specs(l, shared_mod, mod)
    conv_w = lw["ffn_conv_w"]
    pos = jnp.arange(row)[:, None]
    wprev = jnp.where(pos != 0, conv_w[0:1], 0.0)
    wnext = jnp.where(pos != row - 1, conv_w[2:3], 0.0)
    consts = [lw["norm2_g"], lw["fourier_out_w"], lw["lru_out_w"], lw["mix_out_w"], lw["ffn_up_w"],
              wprev, conv_w[1:2], wnext, lw["ffn_conv_b"], lw["ffn_down_w"], lw["final_g"]]
    hidden = lw["ffn_down_w"].shape[0]
    return pl.pallas_call(
        functools.partial(_merge_ffn_body, row=row, final_norm=final_norm),
        grid=(b, l // tl),
        in_specs=[tok(d), tok(f.shape[2]), tok(yr.shape[2]), tok(2 * d), tok(d), mod_spec]
        + [_const_spec(a.shape) for a in consts],
        out_specs=tok(d),
        out_shape=jax.ShapeDtypeStruct((b, l, d), F32),
        scratch_shapes=[pltpu.VMEM((tl, hidden), BF16)],
        compiler_params=_params(2),
        name="merge_ffn",
    )(h, f, yr, g01, ycg, mod, *consts)


def _dft_constants(l, group_width, groups):
    half = l // 2
    k = np.arange(half, dtype=np.int64)
    ang = 2.0 * np.pi * ((k[:, None] * k[None, :]) % half).astype(np.float64) / half
    scale = 1.0 / np.sqrt(float(l) * group_width)
    dc = np.cos(ang) * scale
    ds = -np.sin(ang) * scale
    phi = (2.0 * np.pi * k.astype(np.float64) / l)[:, None] * np.ones((1, group_width * groups))
    twiddles = (jnp.asarray(np.cos(phi), F32), jnp.asarray(np.sin(phi), F32))
    c = np.arange(group_width, dtype=np.int64)
    cang = 2.0 * np.pi * ((c[:, None] * c[None, :]) % group_width).astype(np.float64) / group_width
    eye = np.eye(max(1, min(groups, MXU_TILE // group_width)))
    chan = np.concatenate([np.kron(eye, np.cos(cang)), np.kron(eye, np.sin(cang))], axis=1)
    return (jnp.asarray(dc, BF16), jnp.asarray(ds, BF16)) + twiddles, jnp.asarray(chan, BF16)


def _pair_columns(a, lo, width, chunk):
    two = a[..., lo:lo + 2 * width]
    two = two.reshape(a.shape[:-1] + (2, width // chunk, chunk)).swapaxes(-3, -2).reshape(two.shape)
    return jnp.concatenate([a[..., :lo], two, a[..., lo + 2 * width:]], axis=-1)


def _block_diag(w):
    h, i, j = w.shape
    eye = jnp.eye(h, dtype=w.dtype)
    return (eye[:, None, :, None] * w[:, :, None, :]).reshape(h * i, h * j)


def kernel(x, c, ctx, c_ctx, ada_w, ada_b, norm1_g, norm2_g, in_w, in_b, fourier_out_w, lru_conv_w, lru_conv_b, lru_wa, lru_ba, lru_wx, lru_bx, lru_lam, lru_out_w, conf_conv_w, conf_conv_b, conf_ln_g, conf_ln_b, conf_out_w, mix_out_w, ffn_up_w, ffn_conv_w, ffn_conv_b, ffn_down_w, final_g):
    batch, seq, d = x.shape
    ctx_len = ctx.shape[1]
    depth = ada_w.shape[0]
    fw = fourier_out_w.shape[1]
    lwid = lru_out_w.shape[1]
    cw = conf_out_w.shape[1]
    assert batch + 1 <= MOD_ROWS and seq % GRID_W == 0
    assert lwid == cw

    cs = jnp.zeros((MOD_ROWS, d), F32).at[:batch].set(c).at[batch].set(c_ctx)
    mod = _ada_call(cs, ada_w, ada_b)

    dft = {}
    for length in (seq, ctx_len):
        dft[length], chan_dft = _dft_constants(length, fw // FOURIER_GROUPS, FOURIER_GROUPS)

    off_c = fw + 2 * lwid
    slice_glu_columns = lambda a: _pair_columns(a, off_c, cw, LANES)

    row2 = lambda a: a.reshape(1, -1)
    h_lat, h_ctx = x, ctx
    zero_state = jnp.zeros((batch, 2, lwid), F32)
    for i in range(depth):
        heads_per_tile = MXU_TILE // lru_wa.shape[-1]
        gate_w = jnp.stack([
            jnp.stack([_block_diag(w[t:t + heads_per_tile]) for t in range(0, w.shape[0], heads_per_tile)])
            for w in (lru_wa[i, 0], lru_wx[i, 0], lru_wa[i, 1], lru_wx[i, 1])])
        gate_b = jnp.concatenate([lru_ba[i, 0], lru_bx[i, 0], lru_ba[i, 1], lru_bx[i, 1]])
        lw = dict(fw=fw, lw=lwid, cw=cw,
                  norm1_g=row2(norm1_g[i]), norm2_g=row2(norm2_g[i]),
                  in_w=slice_glu_columns(in_w[i].astype(BF16)), in_b=row2(slice_glu_columns(in_b[i])),
                  chan_dft=chan_dft, lru_conv_w=lru_conv_w[i], lru_conv_b=row2(lru_conv_b[i]),
                  conf_conv_w=conf_conv_w[i], conf_conv_b=row2(conf_conv_b[i]),
                  conf_ln_g=row2(conf_ln_g[i]), conf_ln_b=row2(conf_ln_b[i]), conf_out_w=conf_out_w[i].astype(BF16),
                  lru_gate_w=(0.5 * gate_w).astype(BF16), lru_gate_b=row2(0.5 * gate_b), lru_lam=lru_lam[i],
                  fourier_out_w=fourier_out_w[i].astype(BF16), lru_out_w=lru_out_w[i].astype(BF16),
                  mix_out_w=mix_out_w[i].astype(BF16), ffn_up_w=ffn_up_w[i].astype(BF16),
                  ffn_conv_w=ffn_conv_w[i], ffn_conv_b=row2(ffn_conv_b[i]), ffn_down_w=ffn_down_w[i].astype(BF16),
                  final_g=row2(final_g))
        last = i == depth - 1
        mod_lat = mod[i, :batch].reshape(batch, 1, -1)
        mod_ctx = mod[i, batch:batch + 1].reshape(1, 1, -1)

        flat = lambda a: a.reshape(1, batch * ctx_len, a.shape[-1])
        per_seq = lambda a: a.reshape(batch, ctx_len, a.shape[-1])
        if last:
            xc_c = _lru_input_call(flat(h_ctx), mod_ctx, lw, ctx_len, True)
            _, state_c = _lru_scan_call(per_seq(xc_c), None, zero_state, lw)
        else:
            zcs_c, xc_c, ug_c, g01_c, ycg_c = _mixer_local_call(flat(h_ctx), mod_ctx, lw, ctx_len, True)
            yr_c, state_c = _lru_scan_call(per_seq(xc_c), per_seq(ug_c), zero_state, lw)
            f_c = _fourier_seq_call(per_seq(zcs_c), *dft[ctx_len])
            h_ctx = per_seq(_merge_ffn_call(flat(h_ctx), flat(f_c), flat(yr_c), g01_c, ycg_c, mod_ctx, lw,
                                            ctx_len, True, False))

        zcs, xc, ug, g01, ycg = _mixer_local_call(h_lat, mod_lat, lw, GRID_W, False)
        f = _fourier_seq_call(zcs, *dft[seq])
        yr, _ = _lru_scan_call(xc, ug, state_c, lw)
        h_lat = _merge_ffn_call(h_lat, f, yr, g01, ycg, mod_lat, lw, GRID_W, False, last)
    return h_lat
```

```python
import functools

import numpy as np
import jax
import jax.numpy as jnp
from jax import lax
from jax.experimental import pallas as pl
from jax.experimental.pallas import tpu as pltpu

F32 = jnp.float32
BF16 = jnp.bfloat16

GRID_W = 64
FOURIER_GROUPS = 4
LRU_HEADS = 8
LRU_C = 8.0
N_BRANCH = 3
EPS = 1e-6
LN_EPS = 1e-5

VMEM_LIMIT_BYTES = 56 * 1024 * 1024
SUBLANES = 8
CONV_CHUNK = 64
PAD_ROWS = 16
TOKEN_TILE = 512
HIDDEN_CHUNK = 256
GATE_CHUNK = 256
LANES = 128
MXU_TILE = 256
MOD_ROWS = 24


def _const_spec(shape):
    nd = len(shape)
    return pl.BlockSpec(shape, lambda *_: (0,) * nd, pipeline_mode=pl.Buffered(1))


def _params(n_grid):
    return pltpu.CompilerParams(dimension_semantics=("arbitrary",) * n_grid,
                                vmem_limit_bytes=VMEM_LIMIT_BYTES)


def _dot(a, b):
    return jnp.dot(a, b, preferred_element_type=F32)


def _gelu_tanh(x):
    c1 = float(np.sqrt(2.0 / np.pi))
    half = 0.5 * x
    return half + half * jnp.tanh(x * (c1 + (c1 * 0.044715) * (x * x)))


def _sigmoid(x):
    return 0.5 + 0.5 * jnp.tanh(0.5 * x)


def _silu(x):
    half = 0.5 * x
    return half + half * jnp.tanh(half)


def _rms(x, g):
    return x * lax.rsqrt(jnp.mean(x * x, axis=-1, keepdims=True) + EPS) * g


def _ada_body(c_ref, w_ref, b_ref, o_ref):
    c = c_ref[...]
    s = c * jax.nn.sigmoid(c)
    o_ref[0] = jnp.dot(s, w_ref[0], preferred_element_type=F32,
                       precision=lax.Precision.HIGHEST) + b_ref[0]


def _ada_call(cs, ada_w, ada_b):
    depth, d, n = ada_w.shape
    nblk = 1536
    return pl.pallas_call(
        _ada_body,
        grid=(depth, n // nblk),
        in_specs=[pl.BlockSpec((MOD_ROWS, d), lambda l, j: (0, 0)),
                  pl.BlockSpec((1, d, nblk), lambda l, j: (l, 0, j)),
                  pl.BlockSpec((1, 1, nblk), lambda l, j: (l, 0, j))],
        out_specs=pl.BlockSpec((1, MOD_ROWS, nblk), lambda l, j: (l, 0, j)),
        out_shape=jax.ShapeDtypeStruct((depth, MOD_ROWS, n), F32),
        compiler_params=_params(2),
        name="adaln_mod",
    )(cs, ada_w, ada_b.reshape(depth, 1, n))


def _fill_padded(pad_ref, val, row):
    tl, c = val.shape
    pad_ref[:, PAD_ROWS:PAD_ROWS + row, :] = val.reshape(tl // row, row, c)


def _row_conv_short(pad_ref, val, w_ref, left, row):
    _fill_padded(pad_ref, val, row)
    pieces = []
    for ri in range(val.shape[0] // row):
        for p in range(row // CONV_CHUNK):
            base = PAD_ROWS + p * CONV_CHUNK - left
            acc = pad_ref[ri, base:base + CONV_CHUNK, :] * w_ref[0:1, :]
            for j in range(1, w_ref.shape[0]):
                acc = acc + pad_ref[ri, base + j:base + j + CONV_CHUNK, :] * w_ref[j:j + 1, :]
            pieces.append(acc)
    return jnp.concatenate(pieces, axis=0)


def _tap_offset(left, j, p):
    return divmod(PAD_ROWS - left + j + p * CONV_CHUNK, SUBLANES)


def _build_phase_copies(pad_ref, shift_ref, ri, n_taps, left, lanes=slice(None)):
    span = shift_ref.shape[1]
    for s in sorted({_tap_offset(left, j, 0)[1] for j in range(n_taps)} - {0}):
        shift_ref[s - 1, :, lanes] = pad_ref[ri, s:s + span, lanes]


def _row_conv_wide_chunk(pad_ref, shift_ref, w_ref, left, ri, p, lanes=slice(None)):
    acc = None
    for j in range(w_ref.shape[0]):
        q, s = _tap_offset(left, j, p)
        rows = slice(q * SUBLANES, q * SUBLANES + CONV_CHUNK)
        src = pad_ref[ri, rows, lanes] if s == 0 else shift_ref[s - 1, rows, lanes]
        term = src * w_ref[j:j + 1, lanes]
        acc = term if acc is None else acc + term
    return acc


def _zero_margins(pad_ref, row):
    zeros = jnp.zeros((pad_ref.shape[0], PAD_ROWS, pad_ref.shape[2]), F32)
    pad_ref[:, 0:PAD_ROWS, :] = zeros
    pad_ref[:, PAD_ROWS + row:, :] = zeros


def _normed_input(h_ref, mod_ref, g_ref, d):
    mod = mod_ref[0]
    n = _rms(h_ref[0], g_ref[...]) * (1.0 + mod[:, d:2 * d]) + mod[:, 0:d]
    return n.astype(BF16)


def _mixer_local_body(h_ref, mod_ref, g_ref, w_ref, b_ref, cs_ref, lcw_ref, lcb_ref,
                      ccw_ref, ccb_ref, lng_ref, lnb_ref, cow_ref,
                      zcs_ref, xc_ref, ug_ref, g01_ref, ycg_ref, pad_ref, shift_ref, conv_ref, gate_ref,
                      *, row, dims):
    d, fw, lw, cw = dims
    off_lx = fw
    off_lg = off_lx + lw
    off_c = off_lg + lw
    off_g = off_c + 2 * cw
    tl = h_ref.shape[1]

    _zero_margins(pad_ref, row)
    nb = _normed_input(h_ref, mod_ref, g_ref, d)

    def proj(lo, hi):
        return _dot(nb, w_ref[:, lo:hi]) + b_ref[:, lo:hi]

    ux = proj(off_lx, off_lg)
    units = [(ri, p) for ri in range(tl // row) for p in range(row // CONV_CHUNK)]
    n_taps = lcw_ref.shape[0]
    for c in range(lw // LANES):
        lanes = slice(c * LANES, (c + 1) * LANES)
        pad_ref[:, PAD_ROWS:PAD_ROWS + row, lanes] = ux[:, lanes].reshape(tl // row, row, LANES)
        for u, (ri, p) in enumerate(units):
            if p == 0:
                _build_phase_copies(pad_ref, shift_ref, ri, n_taps, n_taps // 2, lanes)
            xc = _row_conv_wide_chunk(pad_ref, shift_ref, lcw_ref, n_taps // 2, ri, p, lanes) + lcb_ref[:, lanes]
            xc_ref[0, u * CONV_CHUNK:(u + 1) * CONV_CHUNK, lanes] = xc.astype(xc_ref.dtype)

    def fourier_job():
        uf = proj(0, fw).astype(BF16)
        kt = cs_ref.shape[0]
        for t in range(fw // kt):
            z = _dot(uf[:, t * kt:(t + 1) * kt], cs_ref[...]).astype(zcs_ref.dtype)
            zcs_ref[0, :, t * kt:(t + 1) * kt] = z[:, 0:kt]
            zcs_ref[0, :, fw + t * kt:fw + (t + 1) * kt] = z[:, kt:2 * kt]

    def gate_branch_job():
        ug_ref[0] = proj(off_lg, off_c).astype(ug_ref.dtype)

    def gate_job(lo):
        s = _sigmoid(proj(off_g + lo, off_g + lo + GATE_CHUNK))
        if lo < 2 * d:
            g01_ref[0, :, lo:lo + GATE_CHUNK] = s.astype(g01_ref.dtype)
        else:
            gate_ref[:, lo - 2 * d:lo - 2 * d + GATE_CHUNK] = s

    def conformer_job(c):
        lanes = slice(c * LANES, (c + 1) * LANES)
        u = proj(off_c + 2 * c * LANES, off_c + 2 * (c + 1) * LANES)
        glu = u[:, :LANES] * _sigmoid(u[:, LANES:])
        pad_ref[:, PAD_ROWS:PAD_ROWS + row, lanes] = glu.reshape(tl // row, row, LANES)
        n_taps = ccw_ref.shape[0]
        for ui, (ri, p) in enumerate(units):
            if p == 0:
                _build_phase_copies(pad_ref, shift_ref, ri, n_taps, n_taps // 2, lanes)
            v = _row_conv_wide_chunk(pad_ref, shift_ref, ccw_ref, n_taps // 2, ri, p, lanes) + ccb_ref[:, lanes]
            conv_ref[ui * CONV_CHUNK:(ui + 1) * CONV_CHUNK, lanes] = v

    fourier_job()
    gate_branch_job()
    gate_los = list(range(0, N_BRANCH * d, GATE_CHUNK))
    n_slices = cw // LANES
    per_slice = -(-len(gate_los) // n_slices)
    for c in range(n_slices):
        conformer_job(c)
        for lo in gate_los[c * per_slice:(c + 1) * per_slice]:
            gate_job(lo)

    v = conv_ref[...]
    mu = jnp.mean(v, axis=-1, keepdims=True)
    vc = v - mu
    var = jnp.mean(vc * vc, axis=-1, keepdims=True)
    v = vc * lax.rsqrt(var + LN_EPS) * lng_ref[...] + lnb_ref[...]
    yc = _dot(_silu(v).astype(BF16), cow_ref[...])
    ycg_ref[0] = (gate_ref[...] * yc).astype(ycg_ref.dtype)


def _lru_input_body(h_ref, mod_ref, g_ref, w_ref, b_ref, lcw_ref, lcb_ref, xc_ref, pad_ref, *, row, d):
    _zero_margins(pad_ref, row)
    nb = _normed_input(h_ref, mod_ref, g_ref, d)
    ux = _dot(nb, w_ref[...]) + b_ref[...]
    xc = _row_conv_short(pad_ref, ux, lcw_ref, lcw_ref.shape[0] // 2, row) + lcb_ref[...]
    xc_ref[0] = xc.astype(xc_ref.dtype)


def _token_specs(l, shared_mod, mod):
    tl = min(TOKEN_TILE, l)
    mod_map = (lambda bi, ti: (0, 0, 0)) if shared_mod else (lambda bi, ti: (bi, 0, 0))
    tok = lambda width: pl.BlockSpec((1, tl, width), lambda bi, ti: (bi, ti, 0))
    return tl, tok, pl.BlockSpec((1, 1, mod.shape[2]), mod_map)


def _mixer_local_call(h, mod, lw, row, shared_mod):
    b, l, d = h.shape
    fw, lwid, cw = lw["fw"], lw["lw"], lw["cw"]
    tl, tok, mod_spec = _token_specs(l, shared_mod, mod)
    body = functools.partial(_mixer_local_body, row=row, dims=(d, fw, lwid, cw))
    consts = [lw["norm1_g"], lw["in_w"], lw["in_b"], lw["chan_dft"], lw["lru_conv_w"], lw["lru_conv_b"],
              lw["conf_conv_w"], lw["conf_conv_b"], lw["conf_ln_g"], lw["conf_ln_b"], lw["conf_out_w"]]
    widths = [2 * fw, lwid, lwid, 2 * d, d]
    return pl.pallas_call(
        body,
        grid=(b, l // tl),
        in_specs=[tok(d), mod_spec] + [_const_spec(a.shape) for a in consts],
        out_specs=[tok(w) for w in widths],
        out_shape=[jax.ShapeDtypeStruct((b, l, w), BF16) for w in widths],
        scratch_shapes=[pltpu.VMEM((tl // row, row + 2 * PAD_ROWS, max(lwid, cw)), F32),
                        pltpu.VMEM((SUBLANES - 1, row + 2 * PAD_ROWS - SUBLANES, cw), F32),
                        pltpu.VMEM((tl, cw), F32),
                        pltpu.VMEM((tl, d), F32)],
        compiler_params=_params(2),
        name="mixer_local",
    )(h, mod, *consts)


def _lru_input_call(h, mod, lw, row, shared_mod):
    b, l, d = h.shape
    fw, lwid = lw["fw"], lw["lw"]
    tl, tok, mod_spec = _token_specs(l, shared_mod, mod)
    consts = [lw["norm1_g"], lw["in_w"][:, fw:fw + lwid], lw["in_b"][:, fw:fw + lwid],
              lw["lru_conv_w"], lw["lru_conv_b"]]
    return pl.pallas_call(
        functools.partial(_lru_input_body, row=row, d=d),
        grid=(b, l // tl),
        in_specs=[tok(d), mod_spec] + [_const_spec(a.shape) for a in consts],
        out_specs=tok(lwid),
        out_shape=jax.ShapeDtypeStruct((b, l, lwid), BF16),
        scratch_shapes=[pltpu.VMEM((tl // row, row + 2 * PAD_ROWS, lwid), F32)],
        compiler_params=_params(2),
        name="lru_input",
    )(h, mod, *consts)


def _segment_carries(h_fin, p_fin, h0, reverse):
    rows = [None] * SUBLANES
    c = h0
    for s in (range(SUBLANES - 1, -1, -1) if reverse else range(SUBLANES)):
        rows[s] = c
        c = h_fin[:, s:s + 1, :] + p_fin[:, s:s + 1, :] * c
    return jnp.concatenate(rows, axis=1), c


def _lru_scan_body(*refs, chunk, with_output):
    if with_output:
        xc_ref, ug_ref, h0_ref, wg_ref, bg_ref, lam_ref, yr_ref, hfin_ref, af_ref, bf_ref, ab_ref, bb_ref = refs
    else:
        xc_ref, h0_ref, wg_ref, bg_ref, lam_ref, hfin_ref, af_ref, bf_ref, ab_ref, bb_ref = refs
    l, w = xc_ref.shape[1], xc_ref.shape[2]
    groups = w // LANES
    seg = l // SUBLANES
    piece = min(chunk, seg)
    neg = -lam_ref[...]
    softplus = jnp.maximum(neg, 0.0) + jnp.log1p(jnp.exp(-jnp.abs(neg)))
    half_rate = (-0.5 * LRU_C * np.log2(np.e)) * softplus

    def seg_rows(t0):
        s, k0 = divmod(t0, seg)
        return pl.ds(SUBLANES * k0 + s, piece, stride=SUBLANES)

    def scatter(ref, val, t0):
        for q in range(val.shape[0] // piece):
            for g in range(groups):
                ref[g, seg_rows(t0 + q * piece), :] = val[q * piece:(q + 1) * piece, g * LANES:(g + 1) * LANES]

    for i in range(l // chunk):
        rows = slice(i * chunk, (i + 1) * chunk)
        xb = xc_ref[0, rows, :]
        half_x = 0.5 * xb.astype(F32)

        def gate_tanh(gi):
            kt = wg_ref.shape[2]
            pre = [_dot(xb[:, t * kt:(t + 1) * kt], wg_ref[gi, t]) for t in range(w // kt)]
            return jnp.tanh(jnp.concatenate(pre, axis=1) + bg_ref[:, gi * w:(gi + 1) * w])

        for di, (a_ref, b_ref) in enumerate(((af_ref, bf_ref), (ab_ref, bb_ref))):
            rate = half_rate[di:di + 1, :]
            a = jnp.exp2(gate_tanh(2 * di) * rate + rate)
            gated_x = (gate_tanh(2 * di + 1) + 1.0) * half_x
            scatter(a_ref, a, i * chunk)
            y = 1.0 - a * a
            root = jnp.where(y > 0.0, y * lax.rsqrt(y), 0.0)
            scatter(b_ref, root * gated_x, i * chunk)

    def step(k, carry):
        hf, pf, hb, pb = carry
        rf = pl.ds(pl.multiple_of(k * SUBLANES, SUBLANES), SUBLANES)
        rb = pl.ds(pl.multiple_of((seg - 1 - k) * SUBLANES, SUBLANES), SUBLANES)
        a = af_ref[:, rf, :]
        hf = a * hf + bf_ref[:, rf, :]
        pf = a * pf
        bf_ref[:, rf, :] = hf
        af_ref[:, rf, :] = pf
        a = ab_ref[:, rb, :]
        hb = a * hb + bb_ref[:, rb, :]
        pb = a * pb
        bb_ref[:, rb, :] = hb
        ab_ref[:, rb, :] = pb
        return hf, pf, hb, pb

    zeros = jnp.zeros((groups, SUBLANES, LANES), F32)
    ones = jnp.ones((groups, SUBLANES, LANES), F32)
    hf, pf, hb, pb = lax.fori_loop(0, seg, step, (zeros, ones, zeros, ones), unroll=8)

    h0 = h0_ref[0]
    split = lambda row: jnp.stack([row[:, g * LANES:(g + 1) * LANES] for g in range(groups)])
    cf, end_f = _segment_carries(hf, pf, split(h0[0:1, :]), False)
    cb, end_b = _segment_carries(hb, pb, split(h0[1:2, :]), True)
    for g in range(groups):
        hfin_ref[0, 0:1, g * LANES:(g + 1) * LANES] = end_f[g]
        hfin_ref[0, 1:2, g * LANES:(g + 1) * LANES] = end_b[g]

    if with_output:
        blk = min(chunk, l)
        tiles = blk // SUBLANES
        as_tiles = lambda ref, rows: ref[:, rows, :].reshape(groups, tiles, SUBLANES, LANES)
        for r0 in range(0, l, blk):
            rows = slice(r0, r0 + blk)
            total = (as_tiles(bf_ref, rows) + as_tiles(af_ref, rows) * cf[:, None]
                     + as_tiles(bb_ref, rows) + as_tiles(ab_ref, rows) * cb[:, None])
            bf_ref[:, rows, :] = total.reshape(groups, blk, LANES)
        for t0 in range(0, l, piece):
            for g in range(groups):
                lanes = slice(g * LANES, (g + 1) * LANES)
                y = bf_ref[g, seg_rows(t0), :] * _gelu_tanh(ug_ref[0, t0:t0 + piece, lanes].astype(F32))
                yr_ref[0, t0:t0 + piece, lanes] = y.astype(yr_ref.dtype)


def _lru_scan_call(xc, ug, h0, lw):
    b, l, w = xc.shape
    seq = pl.BlockSpec((1, l, w), lambda bi: (bi, 0, 0))
    st = pl.BlockSpec((1, 2, w), lambda bi: (bi, 0, 0))
    consts = [lw["lru_gate_w"], lw["lru_gate_b"], lw["lru_lam"]]
    with_output = ug is not None
    st_shape = jax.ShapeDtypeStruct((b, 2, w), F32)
    out = pl.pallas_call(
        functools.partial(_lru_scan_body, chunk=min(256, l), with_output=with_output),
        grid=(b,),
        in_specs=[seq] * (2 if with_output else 1) + [st] + [_const_spec(a.shape) for a in consts],
        out_specs=[seq, st] if with_output else st,
        out_shape=[jax.ShapeDtypeStruct((b, l, w), BF16), st_shape] if with_output else st_shape,
        scratch_shapes=[pltpu.VMEM((w // LANES, l, LANES), F32)] * 4,
        compiler_params=_params(1),
        name="lru_scan",
    )(*([xc, ug] if with_output else [xc]), h0, *consts)
    return out if with_output else (None, out)


def _fourier_seq_body(z_ref, dc_ref, ds_ref, twc_ref, tws_ref, o_ref, fold_ref, il_ref, *, chunk):
    l, fw = o_ref.shape[1], o_ref.shape[2]
    half = l // 2
    for i in range(half // chunk):
        rows = slice(i * chunk, (i + 1) * chunk)
        lo = z_ref[0, rows, :].astype(F32)
        hi = z_ref[0, half + i * chunk:half + (i + 1) * chunk, :].astype(F32)
        tot, dif = lo + hi, lo - hi
        dre, dim = dif[:, 0:fw], dif[:, fw:2 * fw]
        cos, sin = twc_ref[rows, :], tws_ref[rows, :]
        fold_ref[0, rows, :] = tot[:, 0:fw].astype(fold_ref.dtype)
        fold_ref[1, rows, :] = tot[:, fw:2 * fw].astype(fold_ref.dtype)
        fold_ref[2, rows, :] = (dre * cos - dim * sin).astype(fold_ref.dtype)
        fold_ref[3, rows, :] = (dre * sin + dim * cos).astype(fold_ref.dtype)
    for i in range(half // chunk):
        rows = slice(i * chunk, (i + 1) * chunk)
        even = _dot(dc_ref[rows, :], fold_ref[0]) + _dot(ds_ref[rows, :], fold_ref[1])
        odd = _dot(dc_ref[rows, :], fold_ref[2]) + _dot(ds_ref[rows, :], fold_ref[3])
        for g in range(fw // LANES):
            lanes = slice(g * LANES, (g + 1) * LANES)
            il_ref[g, pl.ds(2 * i * chunk, chunk, stride=2), :] = even[:, lanes]
            il_ref[g, pl.ds(2 * i * chunk + 1, chunk, stride=2), :] = odd[:, lanes]
            out_rows = slice(2 * i * chunk, 2 * (i + 1) * chunk)
            o_ref[0, out_rows, lanes] = il_ref[g, out_rows, :].astype(o_ref.dtype)


def _fourier_seq_call(zcs, dc, ds, twc, tws):
    b, l, w2 = zcs.shape
    fw = w2 // 2
    consts = [dc, ds, twc, tws]
    return pl.pallas_call(
        functools.partial(_fourier_seq_body, chunk=min(256, l // 2)),
        grid=(b,),
        in_specs=[pl.BlockSpec((1, l, w2), lambda bi: (bi, 0, 0))] + [_const_spec(a.shape) for a in consts],
        out_specs=pl.BlockSpec((1, l, fw), lambda bi: (bi, 0, 0)),
        out_shape=jax.ShapeDtypeStruct((b, l, fw), BF16),
        scratch_shapes=[pltpu.VMEM((4, l // 2, fw), BF16), pltpu.VMEM((fw // LANES, l, LANES), F32)],
        compiler_params=_params(1),
        name="fourier_seq",
    )(zcs, *consts)


def _merge_ffn_body(h_ref, f_ref, yr_ref, g01_ref, ycg_ref, mod_ref, g2_ref, fow_ref, low_ref, mow_ref,
                    upw_ref, wprev_ref, wmid_ref, wnext_ref, fcb_ref, dww_ref, fin_ref, o_ref, act_ref,
                    *, row, final_norm):
    tl, d = h_ref.shape[1], h_ref.shape[2]
    hidden = dww_ref.shape[0]
    hc = HIDDEN_CHUNK
    r = tl // row
    mod = mod_ref[0]
    gate1, sh2, sc2, gate2 = (mod[:, 2 * d:3 * d], mod[:, 3 * d:4 * d], mod[:, 4 * d:5 * d], mod[:, 5 * d:6 * d])

    halves = [slice(0, tl)] if tl < 2 * MXU_TILE else [slice(0, tl // 2), slice(tl // 2, tl)]
    h1_parts, nb_parts = [], []
    for rows in halves:
        yf = _dot(f_ref[0, rows, :], fow_ref[...])
        yr = _dot(yr_ref[0, rows, :], low_ref[...])
        merged = g01_ref[0, rows, 0:d] * yf + g01_ref[0, rows, d:2 * d] * yr + ycg_ref[0, rows, :]
        h1_parts.append(h_ref[0, rows, :] + gate1 * _dot(merged.astype(BF16), mow_ref[...]))
        nb_parts.append((_rms(h1_parts[-1], g2_ref[...]) * (1.0 + sc2) + sh2).astype(BF16))
    h1 = jnp.concatenate(h1_parts, axis=0)
    nb = jnp.concatenate(nb_parts, axis=0)

    def conv3(u, lo):
        cols = slice(lo, lo + hc)
        prev = pltpu.roll(u, 1, 0).reshape(r, row, hc) * wprev_ref[:, cols][None]
        nxt = pltpu.roll(u, tl - 1, 0).reshape(r, row, hc) * wnext_ref[:, cols][None]
        out = u.reshape(r, row, hc) * wmid_ref[:, cols][None] + prev + nxt + fcb_ref[:, cols][None]
        return out.reshape(tl, hc)

    for j in range(hidden // hc):
        lo = j * hc
        v = conv3(_dot(nb, upw_ref[:, lo:lo + hc]), lo)
        g = conv3(_dot(nb, upw_ref[:, hidden + lo:hidden + lo + hc]), hidden + lo)
        act_ref[:, lo:lo + hc] = (_silu(g) * v).astype(act_ref.dtype)
    h2 = h1 + gate2 * _dot(act_ref[...], dww_ref[...])
    if final_norm:
        h2 = _rms(h2, fin_ref[...])
    o_ref[0] = h2


def _merge_ffn_call(h, f, yr, g01, ycg, mod, lw, row, shared_mod, final_norm):
    b, l, d = h.shape
    tl, tok, mod_spec = _token_specs(l, shared_mod, mod)
    conv_w = lw["ffn_conv_w"]
    pos = jnp.arange(row)[:, None]
    wprev = jnp.where(pos != 0, conv_w[0:1], 0.0)
    wnext = jnp.where(pos != row - 1, conv_w[2:3], 0.0)
    consts = [lw["norm2_g"], lw["fourier_out_w"], lw["lru_out_w"], lw["mix_out_w"], lw["ffn_up_w"],
              wprev, conv_w[1:2], wnext, lw["ffn_conv_b"], lw["ffn_down_w"], lw["final_g"]]
    hidden = lw["ffn_down_w"].shape[0]
    return pl.pallas_call(
        functools.partial(_merge_ffn_body, row=row, final_norm=final_norm),
        grid=(b, l // tl),
        in_specs=[tok(d), tok(f.shape[2]), tok(yr.shape[2]), tok(2 * d), tok(d), mod_spec]
        + [_const_spec(a.shape) for a in consts],
        out_specs=tok(d),
        out_shape=jax.ShapeDtypeStruct((b, l, d), F32),
        scratch_shapes=[pltpu.VMEM((tl, hidden), BF16)],
        compiler_params=_params(2),
        name="merge_ffn",
    )(h, f, yr, g01, ycg, mod, *consts)


def _dft_constants(l, group_width, groups):
    half = l // 2
    k = np.arange(half, dtype=np.int64)
    ang = 2.0 * np.pi * ((k[:, None] * k[None, :]) % half).astype(np.float64) / half
    scale = 1.0 / np.sqrt(float(l) * group_width)
    dc = np.cos(ang) * scale
    ds = -np.sin(ang) * scale
    phi = (2.0 * np.pi * k.astype(np.float64) / l)[:, None] * np.ones((1, group_width * groups))
    twiddles = (jnp.asarray(np.cos(phi), F32), jnp.asarray(np.sin(phi), F32))
    c = np.arange(group_width, dtype=np.int64)
    cang = 2.0 * np.pi * ((c[:, None] * c[None, :]) % group_width).astype(np.float64) / group_width
    eye = np.eye(max(1, min(groups, MXU_TILE // group_width)))
    chan = np.concatenate([np.kron(eye, np.cos(cang)), np.kron(eye, np.sin(cang))], axis=1)
    return (jnp.asarray(dc, BF16), jnp.asarray(ds, BF16)) + twiddles, jnp.asarray(chan, BF16)


def _pair_columns(a, lo, width, chunk):
    two = a[..., lo:lo + 2 * width]
    two = two.reshape(a.shape[:-1] + (2, width // chunk, chunk)).swapaxes(-3, -2).reshape(two.shape)
    return jnp.concatenate([a[..., :lo], two, a[..., lo + 2 * width:]], axis=-1)


def _block_diag(w):
    h, i, j = w.shape
    eye = jnp.eye(h, dtype=w.dtype)
    return (eye[:, None, :, None] * w[:, :, None, :]).reshape(h * i, h * j)


def kernel(x, c, ctx, c_ctx, ada_w, ada_b, norm1_g, norm2_g, in_w, in_b, fourier_out_w, lru_conv_w, lru_conv_b, lru_wa, lru_ba, lru_wx, lru_bx, lru_lam, lru_out_w, conf_conv_w, conf_conv_b, conf_ln_g, conf_ln_b, conf_out_w, mix_out_w, ffn_up_w, ffn_conv_w, ffn_conv_b, ffn_down_w, final_g):
    batch, seq, d = x.shape
    ctx_len = ctx.shape[1]
    depth = ada_w.shape[0]
    fw = fourier_out_w.shape[1]
    lwid = lru_out_w.shape[1]
    cw = conf_out_w.shape[1]
    assert batch + 1 <= MOD_ROWS and seq % GRID_W == 0
    assert lwid == cw

    cs = jnp.zeros((MOD_ROWS, d), F32).at[:batch].set(c).at[batch].set(c_ctx)
    mod = _ada_call(cs, ada_w, ada_b)

    dft = {}
    for length in (seq, ctx_len):
        dft[length], chan_dft = _dft_constants(length, fw // FOURIER_GROUPS, FOURIER_GROUPS)

    off_c = fw + 2 * lwid
    slice_glu_columns = lambda a: _pair_columns(a, off_c, cw, LANES)

    row2 = lambda a: a.reshape(1, -1)
    h_lat, h_ctx = x, ctx
    zero_state = jnp.zeros((batch, 2, lwid), F32)
    for i in range(depth):
        heads_per_tile = MXU_TILE // lru_wa.shape[-1]
        gate_w = jnp.stack([
            jnp.stack([_block_diag(w[t:t + heads_per_tile]) for t in range(0, w.shape[0], heads_per_tile)])
            for w in (lru_wa[i, 0], lru_wx[i, 0], lru_wa[i, 1], lru_wx[i, 1])])
        gate_b = jnp.concatenate([lru_ba[i, 0], lru_bx[i, 0], lru_ba[i, 1], lru_bx[i, 1]])
        lw = dict(fw=fw, lw=lwid, cw=cw,
                  norm1_g=row2(norm1_g[i]), norm2_g=row2(norm2_g[i]),
                  in_w=slice_glu_columns(in_w[i].astype(BF16)), in_b=row2(slice_glu_columns(in_b[i])),
                  chan_dft=chan_dft, lru_conv_w=lru_conv_w[i], lru_conv_b=row2(lru_conv_b[i]),
                  conf_conv_w=conf_conv_w[i], conf_conv_b=row2(conf_conv_b[i]),
                  conf_ln_g=row2(conf_ln_g[i]), conf_ln_b=row2(conf_ln_b[i]), conf_out_w=conf_out_w[i].astype(BF16),
                  lru_gate_w=(0.5 * gate_w).astype(BF16), lru_gate_b=row2(0.5 * gate_b), lru_lam=lru_lam[i],
                  fourier_out_w=fourier_out_w[i].astype(BF16), lru_out_w=lru_out_w[i].astype(BF16),
                  mix_out_w=mix_out_w[i].astype(BF16), ffn_up_w=ffn_up_w[i].astype(BF16),
                  ffn_conv_w=ffn_conv_w[i], ffn_conv_b=row2(ffn_conv_b[i]), ffn_down_w=ffn_down_w[i].astype(BF16),
                  final_g=row2(final_g))
        last = i == depth - 1
        mod_lat = mod[i, :batch].reshape(batch, 1, -1)
        mod_ctx = mod[i, batch:batch + 1].reshape(1, 1, -1)

        flat = lambda a: a.reshape(1, batch * ctx_len, a.shape[-1])
        per_seq = lambda a: a.reshape(batch, ctx_len, a.shape[-1])
        if last:
            xc_c = _lru_input_call(flat(h_ctx), mod_ctx, lw, ctx_len, True)
            _, state_c = _lru_scan_call(per_seq(xc_c), None, zero_state, lw)
        else:
            zcs_c, xc_c, ug_c, g01_c, ycg_c = _mixer_local_call(flat(h_ctx), mod_ctx, lw, ctx_len, True)
            yr_c, state_c = _lru_scan_call(per_seq(xc_c), per_seq(ug_c), zero_state, lw)
            f_c = _fourier_seq_call(per_seq(zcs_c), *dft[ctx_len])
            h_ctx = per_seq(_merge_ffn_call(flat(h_ctx), flat(f_c), flat(yr_c), g01_c, ycg_c, mod_ctx, lw,
                                            ctx_len, True, False))

        zcs, xc, ug, g01, ycg = _mixer_local_call(h_lat, mod_lat, lw, GRID_W, False)
        f = _fourier_seq_call(zcs, *dft[seq])
        yr, _ = _lru_scan_call(xc, ug, state_c, lw)
        h_lat = _merge_ffn_call(h_lat, f, yr, g01, ycg, mod_lat, lw, GRID_W, False, last)
    return h_lat
```

```python
import functools

import numpy as np
import jax
import jax.numpy as jnp
from jax import lax
from jax.experimental import pallas as pl
from jax.experimental.pallas import tpu as pltpu

F32 = jnp.float32
BF16 = jnp.bfloat16

GRID_W = 64
FOURIER_GROUPS = 4
LRU_HEADS = 8
LRU_C = 8.0
N_BRANCH = 3
EPS = 1e-6
LN_EPS = 1e-5

VMEM_LIMIT_BYTES = 56 * 1024 * 1024
SUBLANES = 8
CONV_CHUNK = 64
PAD_ROWS = 16
TOKEN_TILE = 512
HIDDEN_CHUNK = 256
GATE_CHUNK = 256
LANES = 128
MXU_TILE = 256
MOD_ROWS = 24


def _const_spec(shape):
    nd = len(shape)
    return pl.BlockSpec(shape, lambda *_: (0,) * nd, pipeline_mode=pl.Buffered(1))


def _params(n_grid):
    return pltpu.CompilerParams(dimension_semantics=("arbitrary",) * n_grid,
                                vmem_limit_bytes=VMEM_LIMIT_BYTES)


def _dot(a, b):
    return jnp.dot(a, b, preferred_element_type=F32)


def _gelu_tanh(x):
    c1 = float(np.sqrt(2.0 / np.pi))
    half = 0.5 * x
    return half + half * jnp.tanh(x * (c1 + (c1 * 0.044715) * (x * x)))


def _sigmoid(x):
    return 0.5 + 0.5 * jnp.tanh(0.5 * x)


def _silu(x):
    half = 0.5 * x
    return half + half * jnp.tanh(half)


def _rms(x, g):
    return x * lax.rsqrt(jnp.mean(x * x, axis=-1, keepdims=True) + EPS) * g


def _ada_body(c_ref, w_ref, b_ref, o_ref):
    c = c_ref[...]
    s = c * jax.nn.sigmoid(c)
    o_ref[0] = jnp.dot(s, w_ref[0], preferred_element_type=F32,
                       precision=lax.Precision.HIGHEST) + b_ref[0]


def _ada_call(cs, ada_w, ada_b):
    depth, d, n = ada_w.shape
    nblk = 1536
    return pl.pallas_call(
        _ada_body,
        grid=(depth, n // nblk),
        in_specs=[pl.BlockSpec((MOD_ROWS, d), lambda l, j: (0, 0)),
                  pl.BlockSpec((1, d, nblk), lambda l, j: (l, 0, j)),
                  pl.BlockSpec((1, 1, nblk), lambda l, j: (l, 0, j))],
        out_specs=pl.BlockSpec((1, MOD_ROWS, nblk), lambda l, j: (l, 0, j)),
        out_shape=jax.ShapeDtypeStruct((depth, MOD_ROWS, n), F32),
        compiler_params=_params(2),
        name="adaln_mod",
    )(cs, ada_w, ada_b.reshape(depth, 1, n))


def _fill_padded(pad_ref, val, row):
    tl, c = val.shape
    pad_ref[:, PAD_ROWS:PAD_ROWS + row, :] = val.reshape(tl // row, row, c)


def _row_conv_short(pad_ref, val, w_ref, left, row):
    _fill_padded(pad_ref, val, row)
    pieces = []
    for ri in range(val.shape[0] // row):
        for p in range(row // CONV_CHUNK):
            base = PAD_ROWS + p * CONV_CHUNK - left
            acc = pad_ref[ri, base:base + CONV_CHUNK, :] * w_ref[0:1, :]
            for j in range(1, w_ref.shape[0]):
                acc = acc + pad_ref[ri, base + j:base + j + CONV_CHUNK, :] * w_ref[j:j + 1, :]
            pieces.append(acc)
    return jnp.concatenate(pieces, axis=0)


def _tap_offset(left, j, p):
    return divmod(PAD_ROWS - left + j + p * CONV_CHUNK, SUBLANES)


def _build_phase_copies(pad_ref, shift_ref, ri, n_taps, left, lanes=slice(None)):
    span = shift_ref.shape[1]
    for s in sorted({_tap_offset(left, j, 0)[1] for j in range(n_taps)} - {0}):
        shift_ref[s - 1, :, lanes] = pad_ref[ri, s:s + span, lanes]


def _row_conv_wide_chunk(pad_ref, shift_ref, w_ref, left, ri, p, lanes=slice(None)):
    acc = None
    for j in range(w_ref.shape[0]):
        q, s = _tap_offset(left, j, p)
        rows = slice(q * SUBLANES, q * SUBLANES + CONV_CHUNK)
        src = pad_ref[ri, rows, lanes] if s == 0 else shift_ref[s - 1, rows, lanes]
        term = src * w_ref[j:j + 1, lanes]
        acc = term if acc is None else acc + term
    return acc


def _zero_margins(pad_ref, row):
    zeros = jnp.zeros((pad_ref.shape[0], PAD_ROWS, pad_ref.shape[2]), F32)
    pad_ref[:, 0:PAD_ROWS, :] = zeros
    pad_ref[:, PAD_ROWS + row:, :] = zeros


def _normed_input(h_ref, mod_ref, g_ref, d):
    mod = mod_ref[0]
    n = _rms(h_ref[0], g_ref[...]) * (1.0 + mod[:, d:2 * d]) + mod[:, 0:d]
    return n.astype(BF16)


def _mixer_local_body(h_ref, mod_ref, g_ref, w_ref, b_ref, cs_ref, lcw_ref, lcb_ref,
                      ccw_ref, ccb_ref, lng_ref, lnb_ref, cow_ref,
                      zcs_ref, xc_ref, ug_ref, g01_ref, ycg_ref, pad_ref, shift_ref, conv_ref, gate_ref,
                      *, row, dims):
    d, fw, lw, cw = dims
    off_lx = fw
    off_lg = off_lx + lw
    off_c = off_lg + lw
    off_g = off_c + 2 * cw
    tl = h_ref.shape[1]

    _zero_margins(pad_ref, row)
    nb = _normed_input(h_ref, mod_ref, g_ref, d)

    def proj(lo, hi):
        return _dot(nb, w_ref[:, lo:hi]) + b_ref[:, lo:hi]

    ux = proj(off_lx, off_lg)
    units = [(ri, p) for ri in range(tl // row) for p in range(row // CONV_CHUNK)]
    n_taps = lcw_ref.shape[0]
    for c in range(lw // LANES):
        lanes = slice(c * LANES, (c + 1) * LANES)
        pad_ref[:, PAD_ROWS:PAD_ROWS + row, lanes] = ux[:, lanes].reshape(tl // row, row, LANES)
        for u, (ri, p) in enumerate(units):
            if p == 0:
                _build_phase_copies(pad_ref, shift_ref, ri, n_taps, n_taps // 2, lanes)
            xc = _row_conv_wide_chunk(pad_ref, shift_ref, lcw_ref, n_taps // 2, ri, p, lanes) + lcb_ref[:, lanes]
            xc_ref[0, u * CONV_CHUNK:(u + 1) * CONV_CHUNK, lanes] = xc.astype(xc_ref.dtype)

    def fourier_job():
        uf = proj(0, fw).astype(BF16)
        kt = cs_ref.shape[0]
        for t in range(fw // kt):
            z = _dot(uf[:, t * kt:(t + 1) * kt], cs_ref[...]).astype(zcs_ref.dtype)
            zcs_ref[0, :, t * kt:(t + 1) * kt] = z[:, 0:kt]
            zcs_ref[0, :, fw + t * kt:fw + (t + 1) * kt] = z[:, kt:2 * kt]

    def gate_branch_job():
        ug_ref[0] = proj(off_lg, off_c).astype(ug_ref.dtype)

    def gate_job(lo):
        s = _sigmoid(proj(off_g + lo, off_g + lo + GATE_CHUNK))
        if lo < 2 * d:
            g01_ref[0, :, lo:lo + GATE_CHUNK] = s.astype(g01_ref.dtype)
        else:
            gate_ref[:, lo - 2 * d:lo - 2 * d + GATE_CHUNK] = s

    def conformer_job(c):
        lanes = slice(c * LANES, (c + 1) * LANES)
        u = proj(off_c + 2 * c * LANES, off_c + 2 * (c + 1) * LANES)
        glu = u[:, :LANES] * _sigmoid(u[:, LANES:])
        pad_ref[:, PAD_ROWS:PAD_ROWS + row, lanes] = glu.reshape(tl // row, row, LANES)
        n_taps = ccw_ref.shape[0]
        for ui, (ri, p) in enumerate(units):
            if p == 0:
                _build_phase_copies(pad_ref, shift_ref, ri, n_taps, n_taps // 2, lanes)
            v = _row_conv_wide_chunk(pad_ref, shift_ref, ccw_ref, n_taps // 2, ri, p, lanes) + ccb_ref[:, lanes]
            conv_ref[ui * CONV_CHUNK:(ui + 1) * CONV_CHUNK, lanes] = v

    fourier_job()
    gate_branch_job()
    gate_los = list(range(0, N_BRANCH * d, GATE_CHUNK))
    n_slices = cw // LANES
    per_slice = -(-len(gate_los) // n_slices)
    for c in range(n_slices):
        conformer_job(c)
        for lo in gate_los[c * per_slice:(c + 1) * per_slice]:
            gate_job(lo)

    v = conv_ref[...]
    mu = jnp.mean(v, axis=-1, keepdims=True)
    vc = v - mu
    var = jnp.mean(vc * vc, axis=-1, keepdims=True)
    v = vc * lax.rsqrt(var + LN_EPS) * lng_ref[...] + lnb_ref[...]
    yc = _dot(_silu(v).astype(BF16), cow_ref[...])
    ycg_ref[0] = (gate_ref[...] * yc).astype(ycg_ref.dtype)


def _lru_input_body(h_ref, mod_ref, g_ref, w_ref, b_ref, lcw_ref, lcb_ref, xc_ref, pad_ref, *, row, d):
    _zero_margins(pad_ref, row)
    nb = _normed_input(h_ref, mod_ref, g_ref, d)
    ux = _dot(nb, w_ref[...]) + b_ref[...]
    xc = _row_conv_short(pad_ref, ux, lcw_ref, lcw_ref.shape[0] // 2, row) + lcb_ref[...]
    xc_ref[0] = xc.astype(xc_ref.dtype)


def _token_specs(l, shared_mod, mod):
    tl = min(TOKEN_TILE, l)
    mod_map = (lambda bi, ti: (0, 0, 0)) if shared_mod else (lambda bi, ti: (bi, 0, 0))
    tok = lambda width: pl.BlockSpec((1, tl, width), lambda bi, ti: (bi, ti, 0))
    return tl, tok, pl.BlockSpec((1, 1, mod.shape[2]), mod_map)


def _mixer_local_call(h, mod, lw, row, shared_mod):
    b, l, d = h.shape
    fw, lwid, cw = lw["fw"], lw["lw"], lw["cw"]
    tl, tok, mod_spec = _token_specs(l, shared_mod, mod)
    body = functools.partial(_mixer_local_body, row=row, dims=(d, fw, lwid, cw))
    consts = [lw["norm1_g"], lw["in_w"], lw["in_b"], lw["chan_dft"], lw["lru_conv_w"], lw["lru_conv_b"],
              lw["conf_conv_w"], lw["conf_conv_b"], lw["conf_ln_g"], lw["conf_ln_b"], lw["conf_out_w"]]
    widths = [2 * fw, lwid, lwid, 2 * d, d]
    return pl.pallas_call(
        body,
        grid=(b, l // tl),
        in_specs=[tok(d), mod_spec] + [_const_spec(a.shape) for a in consts],
        out_specs=[tok(w) for w in widths],
        out_shape=[jax.ShapeDtypeStruct((b, l, w), BF16) for w in widths],
        scratch_shapes=[pltpu.VMEM((tl // row, row + 2 * PAD_ROWS, max(lwid, cw)), F32),
                        pltpu.VMEM((SUBLANES - 1, row + 2 * PAD_ROWS - SUBLANES, cw), F32),
                        pltpu.VMEM((tl, cw), F32),
                        pltpu.VMEM((tl, d), F32)],
        compiler_params=_params(2),
        name="mixer_local",
    )(h, mod, *consts)


def _lru_input_call(h, mod, lw, row, shared_mod):
    b, l, d = h.shape
    fw, lwid = lw["fw"], lw["lw"]
    tl, tok, mod_spec = _token_specs(l, shared_mod, mod)
    consts = [lw["norm1_g"], lw["in_w"][:, fw:fw + lwid], lw["in_b"][:, fw:fw + lwid],
              lw["lru_conv_w"], lw["lru_conv_b"]]
    return pl.pallas_call(
        functools.partial(_lru_input_body, row=row, d=d),
        grid=(b, l // tl),
        in_specs=[tok(d), mod_spec] + [_const_spec(a.shape) for a in consts],
        out_specs=tok(lwid),
        out_shape=jax.ShapeDtypeStruct((b, l, lwid), BF16),
        scratch_shapes=[pltpu.VMEM((tl // row, row + 2 * PAD_ROWS, lwid), F32)],
        compiler_params=_params(2),
        name="lru_input",
    )(h, mod, *consts)


def _segment_carries(h_fin, p_fin, h0, reverse):
    rows = [None] * SUBLANES
    c = h0
    for s in (range(SUBLANES - 1, -1, -1) if reverse else range(SUBLANES)):
        rows[s] = c
        c = h_fin[:, s:s + 1, :] + p_fin[:, s:s + 1, :] * c
    return jnp.concatenate(rows, axis=1), c


def _lru_scan_body(*refs, chunk, with_output):
    if with_output:
        xc_ref, ug_ref, h0_ref, wg_ref, bg_ref, lam_ref, yr_ref, hfin_ref, af_ref, bf_ref, ab_ref, bb_ref = refs
    else:
        xc_ref, h0_ref, wg_ref, bg_ref, lam_ref, hfin_ref, af_ref, bf_ref, ab_ref, bb_ref = refs
    l, w = xc_ref.shape[1], xc_ref.shape[2]
    groups = w // LANES
    seg = l // SUBLANES
    piece = min(chunk, seg)
    neg = -lam_ref[...]
    softplus = jnp.maximum(neg, 0.0) + jnp.log1p(jnp.exp(-jnp.abs(neg)))
    half_rate = (-0.5 * LRU_C * np.log2(np.e)) * softplus

    def seg_rows(t0):
        s, k0 = divmod(t0, seg)
        return pl.ds(SUBLANES * k0 + s, piece, stride=SUBLANES)

    def scatter(ref, val, t0):
        for q in range(val.shape[0] // piece):
            for g in range(groups):
                ref[g, seg_rows(t0 + q * piece), :] = val[q * piece:(q + 1) * piece, g * LANES:(g + 1) * LANES]

    for i in range(l // chunk):
        rows = slice(i * chunk, (i + 1) * chunk)
        xb = xc_ref[0, rows, :]
        half_x = 0.5 * xb.astype(F32)

        def gate_tanh(gi):
            kt = wg_ref.shape[2]
            pre = [_dot(xb[:, t * kt:(t + 1) * kt], wg_ref[gi, t]) for t in range(w // kt)]
            return jnp.tanh(jnp.concatenate(pre, axis=1) + bg_ref[:, gi * w:(gi + 1) * w])

        for di, (a_ref, b_ref) in enumerate(((af_ref, bf_ref), (ab_ref, bb_ref))):
            rate = half_rate[di:di + 1, :]
            a = jnp.exp2(gate_tanh(2 * di) * rate + rate)
            gated_x = (gate_tanh(2 * di + 1) + 1.0) * half_x
            scatter(a_ref, a, i * chunk)
            y = 1.0 - a * a
            root = jnp.where(y > 0.0, y * lax.rsqrt(y), 0.0)
            scatter(b_ref, root * gated_x, i * chunk)

    def step(k, carry):
        hf, pf, hb, pb = carry
        rf = pl.ds(pl.multiple_of(k * SUBLANES, SUBLANES), SUBLANES)
        rb = pl.ds(pl.multiple_of((seg - 1 - k) * SUBLANES, SUBLANES), SUBLANES)
        a = af_ref[:, rf, :]
        hf = a * hf + bf_ref[:, rf, :]
        pf = a * pf
        bf_ref[:, rf, :] = hf
        af_ref[:, rf, :] = pf
        a = ab_ref[:, rb, :]
        hb = a * hb + bb_ref[:, rb, :]
        pb = a * pb
        bb_ref[:, rb, :] = hb
        ab_ref[:, rb, :] = pb
        return hf, pf, hb, pb

    zeros = jnp.zeros((groups, SUBLANES, LANES), F32)
    ones = jnp.ones((groups, SUBLANES, LANES), F32)
    hf, pf, hb, pb = lax.fori_loop(0, seg, step, (zeros, ones, zeros, ones), unroll=8)

    h0 = h0_ref[0]
    split = lambda row: jnp.stack([row[:, g * LANES:(g + 1) * LANES] for g in range(groups)])
    cf, end_f = _segment_carries(hf, pf, split(h0[0:1, :]), False)
    cb, end_b = _segment_carries(hb, pb, split(h0[1:2, :]), True)
    for g in range(groups):
        hfin_ref[0, 0:1, g * LANES:(g + 1) * LANES] = end_f[g]
        hfin_ref[0, 1:2, g * LANES:(g + 1) * LANES] = end_b[g]

    if with_output:
        blk = min(chunk, l)
        tiles = blk // SUBLANES
        as_tiles = lambda ref, rows: ref[:, rows, :].reshape(groups, tiles, SUBLANES, LANES)
        for r0 in range(0, l, blk):
            rows = slice(r0, r0 + blk)
            total = (as_tiles(bf_ref, rows) + as_tiles(af_ref, rows) * cf[:, None]
                     + as_tiles(bb_ref, rows) + as_tiles(ab_ref, rows) * cb[:, None])
            bf_ref[:, rows, :] = total.reshape(groups, blk, LANES)
        for t0 in range(0, l, piece):
            for g in range(groups):
                lanes = slice(g * LANES, (g + 1) * LANES)
                y = bf_ref[g, seg_rows(t0), :] * _gelu_tanh(ug_ref[0, t0:t0 + piece, lanes].astype(F32))
                yr_ref[0, t0:t0 + piece, lanes] = y.astype(yr_ref.dtype)


def _lru_scan_call(xc, ug, h0, lw):
    b, l, w = xc.shape
    seq = pl.BlockSpec((1, l, w), lambda bi: (bi, 0, 0))
    st = pl.BlockSpec((1, 2, w), lambda bi: (bi, 0, 0))
    consts = [lw["lru_gate_w"], lw["lru_gate_b"], lw["lru_lam"]]
    with_output = ug is not None
    st_shape = jax.ShapeDtypeStruct((b, 2, w), F32)
    out = pl.pallas_call(
        functools.partial(_lru_scan_body, chunk=min(256, l), with_output=with_output),
        grid=(b,),
        in_specs=[seq] * (2 if with_output else 1) + [st] + [_const_spec(a.shape) for a in consts],
        out_specs=[seq, st] if with_output else st,
        out_shape=[jax.ShapeDtypeStruct((b, l, w), BF16), st_shape] if with_output else st_shape,
        scratch_shapes=[pltpu.VMEM((w // LANES, l, LANES), F32)] * 4,
        compiler_params=_params(1),
        name="lru_scan",
    )(*([xc, ug] if with_output else [xc]), h0, *consts)
    return out if with_output else (None, out)


def _fourier_seq_body(z_ref, dc_ref, ds_ref, twc_ref, tws_ref, o_ref, fold_ref, il_ref, *, chunk):
    l, fw = o_ref.shape[1], o_ref.shape[2]
    half = l // 2
    for i in range(half // chunk):
        rows = slice(i * chunk, (i + 1) * chunk)
        lo = z_ref[0, rows, :].astype(F32)
        hi = z_ref[0, half + i * chunk:half + (i + 1) * chunk, :].astype(F32)
        tot, dif = lo + hi, lo - hi
        dre, dim = dif[:, 0:fw], dif[:, fw:2 * fw]
        cos, sin = twc_ref[rows, :], tws_ref[rows, :]
        fold_ref[0, rows, :] = tot[:, 0:fw].astype(fold_ref.dtype)
        fold_ref[1, rows, :] = tot[:, fw:2 * fw].astype(fold_ref.dtype)
        fold_ref[2, rows, :] = (dre * cos - dim * sin).astype(fold_ref.dtype)
        fold_ref[3, rows, :] = (dre * sin + dim * cos).astype(fold_ref.dtype)
    for i in range(half // chunk):
        rows = slice(i * chunk, (i + 1) * chunk)
        even = _dot(dc_ref[rows, :], fold_ref[0]) + _dot(ds_ref[rows, :], fold_ref[1])
        odd = _dot(dc_ref[rows, :], fold_ref[2]) + _dot(ds_ref[rows, :], fold_ref[3])
        for g in range(fw // LANES):
            lanes = slice(g * LANES, (g + 1) * LANES)
            il_ref[g, pl.ds(2 * i * chunk, chunk, stride=2), :] = even[:, lanes]
            il_ref[g, pl.ds(2 * i * chunk + 1, chunk, stride=2), :] = odd[:, lanes]
            out_rows = slice(2 * i * chunk, 2 * (i + 1) * chunk)
            o_ref[0, out_rows, lanes] = il_ref[g, out_rows, :].astype(o_ref.dtype)


def _fourier_seq_call(zcs, dc, ds, twc, tws):
    b, l, w2 = zcs.shape
    fw = w2 // 2
    consts = [dc, ds, twc, tws]
    return pl.pallas_call(
        functools.partial(_fourier_seq_body, chunk=min(256, l // 2)),
        grid=(b,),
        in_specs=[pl.BlockSpec((1, l, w2), lambda bi: (bi, 0, 0))] + [_const_spec(a.shape) for a in consts],
        out_specs=pl.BlockSpec((1, l, fw), lambda bi: (bi, 0, 0)),
        out_shape=jax.ShapeDtypeStruct((b, l, fw), BF16),
        scratch_shapes=[pltpu.VMEM((4, l // 2, fw), BF16), pltpu.VMEM((fw // LANES, l, LANES), F32)],
        compiler_params=_params(1),
        name="fourier_seq",
    )(zcs, *consts)


def _seq_mix_body(xc_ref, ug_ref, z_ref, h0_ref, wg_ref, bg_ref, lam_ref, dc_ref, ds_ref, twc_ref, tws_ref,
                  yr_ref, f_ref, hfin_ref, af_ref, bf_ref, ab_ref, bb_ref, fold_ref, il_ref, *, chunk):
    _fourier_seq_body(z_ref, dc_ref, ds_ref, twc_ref, tws_ref, f_ref, fold_ref, il_ref,
                      chunk=min(chunk, z_ref.shape[1] // 2))
    _lru_scan_body(xc_ref, ug_ref, h0_ref, wg_ref, bg_ref, lam_ref, yr_ref, hfin_ref,
                   af_ref, bf_ref, ab_ref, bb_ref, chunk=chunk, with_output=True)


def _seq_mix_call(xc, ug, zcs, h0, lw, dft):
    b, l, w = xc.shape
    fw = zcs.shape[2] // 2
    seq_in = lambda width: pl.BlockSpec((1, l, width), lambda bi: (bi, 0, 0), pipeline_mode=pl.Buffered(1))
    seq_out = lambda width: pl.BlockSpec((1, l, width), lambda bi: (bi, 0, 0))
    st = pl.BlockSpec((1, 2, w), lambda bi: (bi, 0, 0))
    consts = [lw["lru_gate_w"], lw["lru_gate_b"], lw["lru_lam"]] + list(dft)
    return pl.pallas_call(
        functools.partial(_seq_mix_body, chunk=min(256, l)),
        grid=(b,),
        in_specs=[seq_in(w), seq_in(w), seq_in(2 * fw), st] + [_const_spec(a.shape) for a in consts],
        out_specs=[seq_out(w), seq_out(fw), st],
        out_shape=[jax.ShapeDtypeStruct((b, l, w), BF16), jax.ShapeDtypeStruct((b, l, fw), BF16),
                   jax.ShapeDtypeStruct((b, 2, w), F32)],
        scratch_shapes=[pltpu.VMEM((w // LANES, l, LANES), F32)] * 4
        + [pltpu.VMEM((4, l // 2, fw), BF16), pltpu.VMEM((fw // LANES, l, LANES), F32)],
        compiler_params=_params(1),
        name="seq_mix",
    )(xc, ug, zcs, h0, *consts)


def _merge_ffn_body(h_ref, f_ref, yr_ref, g01_ref, ycg_ref, mod_ref, g2_ref, fow_ref, low_ref, mow_ref,
                    upw_ref, wprev_ref, wmid_ref, wnext_ref, fcb_ref, dww_ref, fin_ref, o_ref, act_ref,
                    *, row, final_norm):
    tl, d = h_ref.shape[1], h_ref.shape[2]
    hidden = dww_ref.shape[0]
    hc = HIDDEN_CHUNK
    r = tl // row
    mod = mod_ref[0]
    gate1, sh2, sc2, gate2 = (mod[:, 2 * d:3 * d], mod[:, 3 * d:4 * d], mod[:, 4 * d:5 * d], mod[:, 5 * d:6 * d])

    halves = [slice(0, tl)] if tl < 2 * MXU_TILE else [slice(0, tl // 2), slice(tl // 2, tl)]
    h1_parts, nb_parts = [], []
    for rows in halves:
        yf = _dot(f_ref[0, rows, :], fow_ref[...])
        yr = _dot(yr_ref[0, rows, :], low_ref[...])
        merged = g01_ref[0, rows, 0:d] * yf + g01_ref[0, rows, d:2 * d] * yr + ycg_ref[0, rows, :]
        h1_parts.append(h_ref[0, rows, :] + gate1 * _dot(merged.astype(BF16), mow_ref[...]))
        nb_parts.append((_rms(h1_parts[-1], g2_ref[...]) * (1.0 + sc2) + sh2).astype(BF16))
    h1 = jnp.concatenate(h1_parts, axis=0)
    nb = jnp.concatenate(nb_parts, axis=0)

    def conv3(u, lo):
        cols = slice(lo, lo + hc)
        prev = pltpu.roll(u, 1, 0).reshape(r, row, hc) * wprev_ref[:, cols][None]
        nxt = pltpu.roll(u, tl - 1, 0).reshape(r, row, hc) * wnext_ref[:, cols][None]
        out = u.reshape(r, row, hc) * wmid_ref[:, cols][None] + prev + nxt + fcb_ref[:, cols][None]
        return out.reshape(tl, hc)

    for j in range(hidden // hc):
        lo = j * hc
        v = conv3(_dot(nb, upw_ref[:, lo:lo + hc]), lo)
        g = conv3(_dot(nb, upw_ref[:, hidden + lo:hidden + lo + hc]), hidden + lo)
        act_ref[:, lo:lo + hc] = (_silu(g) * v).astype(act_ref.dtype)
    h2 = h1 + gate2 * _dot(act_ref[...], dww_ref[...])
    if final_norm:
        h2 = _rms(h2, fin_ref[...])
    o_ref[0] = h2


def _merge_ffn_call(h, f, yr, g01, ycg, mod, lw, row, shared_mod, final_norm):
    b, l, d = h.shape
    tl, tok, mod_spec = _token_specs(l, shared_mod, mod)
    conv_w = lw["ffn_conv_w"]
    pos = jnp.arange(row)[:, None]
    wprev = jnp.where(pos != 0, conv_w[0:1], 0.0)
    wnext = jnp.where(pos != row - 1, conv_w[2:3], 0.0)
    consts = [lw["norm2_g"], lw["fourier_out_w"], lw["lru_out_w"], lw["mix_out_w"], lw["ffn_up_w"],
              wprev, conv_w[1:2], wnext, lw["ffn_conv_b"], lw["ffn_down_w"], lw["final_g"]]
    hidden = lw["ffn_down_w"].shape[0]
    return pl.pallas_call(
        functools.partial(_merge_ffn_body, row=row, final_norm=final_norm),
        grid=(b, l // tl),
        in_specs=[tok(d), tok(f.shape[2]), tok(yr.shape[2]), tok(2 * d), tok(d), mod_spec]
        + [_const_spec(a.shape) for a in consts],
        out_specs=tok(d),
        out_shape=jax.ShapeDtypeStruct((b, l, d), F32),
        scratch_shapes=[pltpu.VMEM((tl, hidden), BF16)],
        compiler_params=_params(2),
        name="merge_ffn",
    )(h, f, yr, g01, ycg, mod, *consts)


def _dft_constants(l, group_width, groups):
    half = l // 2
    k = np.arange(half, dtype=np.int64)
    ang = 2.0 * np.pi * ((k[:, None] * k[None, :]) % half).astype(np.float64) / half
    scale = 1.0 / np.sqrt(float(l) * group_width)
    dc = np.cos(ang) * scale
    ds = -np.sin(ang) * scale
    phi = (2.0 * np.pi * k.astype(np.float64) / l)[:, None] * np.ones((1, group_width * groups))
    twiddles = (jnp.asarray(np.cos(phi), F32), jnp.asarray(np.sin(phi), F32))
    c = np.arange(group_width, dtype=np.int64)
    cang = 2.0 * np.pi * ((c[:, None] * c[None, :]) % group_width).astype(np.float64) / group_width
    eye = np.eye(max(1, min(groups, MXU_TILE // group_width)))
    chan = np.concatenate([np.kron(eye, np.cos(cang)), np.kron(eye, np.sin(cang))], axis=1)
    return (jnp.asarray(dc, BF16), jnp.asarray(ds, BF16)) + twiddles, jnp.asarray(chan, BF16)


def _pair_columns(a, lo, width, chunk):
    two = a[..., lo:lo + 2 * width]
    two = two.reshape(a.shape[:-1] + (2, width // chunk, chunk)).swapaxes(-3, -2).reshape(two.shape)
    return jnp.concatenate([a[..., :lo], two, a[..., lo + 2 * width:]], axis=-1)


def _block_diag(w):
    h, i, j = w.shape
    eye = jnp.eye(h, dtype=w.dtype)
    return (eye[:, None, :, None] * w[:, :, None, :]).reshape(h * i, h * j)


def kernel(x, c, ctx, c_ctx, ada_w, ada_b, norm1_g, norm2_g, in_w, in_b, fourier_out_w, lru_conv_w, lru_conv_b, lru_wa, lru_ba, lru_wx, lru_bx, lru_lam, lru_out_w, conf_conv_w, conf_conv_b, conf_ln_g, conf_ln_b, conf_out_w, mix_out_w, ffn_up_w, ffn_conv_w, ffn_conv_b, ffn_down_w, final_g):
    batch, seq, d = x.shape
    ctx_len = ctx.shape[1]
    depth = ada_w.shape[0]
    fw = fourier_out_w.shape[1]
    lwid = lru_out_w.shape[1]
    cw = conf_out_w.shape[1]
    assert batch + 1 <= MOD_ROWS and seq % GRID_W == 0
    assert lwid == cw

    cs = jnp.zeros((MOD_ROWS, d), F32).at[:batch].set(c).at[batch].set(c_ctx)
    mod = _ada_call(cs, ada_w, ada_b)

    dft = {}
    for length in (seq, ctx_len):
        dft[length], chan_dft = _dft_constants(length, fw // FOURIER_GROUPS, FOURIER_GROUPS)

    off_c = fw + 2 * lwid
    slice_glu_columns = lambda a: _pair_columns(a, off_c, cw, LANES)

    row2 = lambda a: a.reshape(1, -1)
    h_lat, h_ctx = x, ctx
    zero_state = jnp.zeros((batch, 2, lwid), F32)
    for i in range(depth):
        heads_per_tile = MXU_TILE // lru_wa.shape[-1]
        gate_w = jnp.stack([
            jnp.stack([_block_diag(w[t:t + heads_per_tile]) for t in range(0, w.shape[0], heads_per_tile)])
            for w in (lru_wa[i, 0], lru_wx[i, 0], lru_wa[i, 1], lru_wx[i, 1])])
        gate_b = jnp.concatenate([lru_ba[i, 0], lru_bx[i, 0], lru_ba[i, 1], lru_bx[i, 1]])
        lw = dict(fw=fw, lw=lwid, cw=cw,
                  norm1_g=row2(norm1_g[i]), norm2_g=row2(norm2_g[i]),
                  in_w=slice_glu_columns(in_w[i].astype(BF16)), in_b=row2(slice_glu_columns(in_b[i])),
                  chan_dft=chan_dft, lru_conv_w=lru_conv_w[i], lru_conv_b=row2(lru_conv_b[i]),
                  conf_conv_w=conf_conv_w[i], conf_conv_b=row2(conf_conv_b[i]),
                  conf_ln_g=row2(conf_ln_g[i]), conf_ln_b=row2(conf_ln_b[i]), conf_out_w=conf_out_w[i].astype(BF16),
                  lru_gate_w=(0.5 * gate_w).astype(BF16), lru_gate_b=row2(0.5 * gate_b), lru_lam=lru_lam[i],
                  fourier_out_w=fourier_out_w[i].astype(BF16), lru_out_w=lru_out_w[i].astype(BF16),
                  mix_out_w=mix_out_w[i].astype(BF16), ffn_up_w=ffn_up_w[i].astype(BF16),
                  ffn_conv_w=ffn_conv_w[i], ffn_conv_b=row2(ffn_conv_b[i]), ffn_down_w=ffn_down_w[i].astype(BF16),
                  final_g=row2(final_g))
        last = i == depth - 1
        mod_lat = mod[i, :batch].reshape(batch, 1, -1)
        mod_ctx = mod[i, batch:batch + 1].reshape(1, 1, -1)

        flat = lambda a: a.reshape(1, batch * ctx_len, a.shape[-1])
        per_seq = lambda a: a.reshape(batch, ctx_len, a.shape[-1])
        if last:
            xc_c = _lru_input_call(flat(h_ctx), mod_ctx, lw, ctx_len, True)
            _, state_c = _lru_scan_call(per_seq(xc_c), None, zero_state, lw)
        else:
            zcs_c, xc_c, ug_c, g01_c, ycg_c = _mixer_local_call(flat(h_ctx), mod_ctx, lw, ctx_len, True)
            yr_c, state_c = _lru_scan_call(per_seq(xc_c), per_seq(ug_c), zero_state, lw)
            f_c = _fourier_seq_call(per_seq(zcs_c), *dft[ctx_len])
            h_ctx = per_seq(_merge_ffn_call(flat(h_ctx), flat(f_c), flat(yr_c), g01_c, ycg_c, mod_ctx, lw,
                                            ctx_len, True, False))

        zcs, xc, ug, g01, ycg = _mixer_local_call(h_lat, mod_lat, lw, GRID_W, False)
        yr, f, _ = _seq_mix_call(xc, ug, zcs, state_c, lw, dft[seq])
        h_lat = _merge_ffn_call(h_lat, f, yr, g01, ycg, mod_lat, lw, GRID_W, False, last)
    return h_lat
```
